```python
import math
import jax, jax.numpy as jnp
from jax import lax
import numpy as np

D_MODEL = 1024
BATCH = 8
SEQ = 2048
DEPTH = 1

PLE_DIM = 256
D_MIX = 2 * D_MODEL
LRU_WIDTH = D_MIX // 2
LRU_HEADS = 16
LRU_HEAD_DIM = LRU_WIDTH // LRU_HEADS
LRU_C = 8.0
SSD_WIDTH = D_MIX - LRU_WIDTH
SSD_HEAD_DIM = 64
SSD_HEADS = SSD_WIDTH // SSD_HEAD_DIM
SSD_GROUPS = 4
SSD_STATE = 128
SSD_CHUNK = 128
SSD_XBC = SSD_WIDTH + 2 * SSD_GROUPS * SSD_STATE
CONV_WIDTH = 4
D_FF = 4 * D_MODEL
ALPHA = (2.0 * DEPTH) ** 0.25
BETA = (8.0 * DEPTH) ** -0.25
LN_EPS = 1e-5
RMS_EPS = 1e-5
IN_SPLIT_POINTS = (LRU_WIDTH, 2 * LRU_WIDTH, 2 * LRU_WIDTH + SSD_WIDTH, 2 * LRU_WIDTH + SSD_WIDTH + SSD_XBC)
D_IN_PROJ = 2 * LRU_WIDTH + SSD_WIDTH + SSD_XBC + SSD_HEADS

kernel_name = "hymba_style_rglru_ssd_deepnorm_block"


def layer_norm(x, g, b):
    xf = x.astype(jnp.float32)
    mu = jnp.mean(xf, axis=-1, keepdims=True)
    xc = xf - mu
    var = jnp.mean(xc * xc, axis=-1, keepdims=True)
    y = xc * lax.rsqrt(var + LN_EPS) * g.astype(jnp.float32) + b.astype(jnp.float32)
    return y.astype(x.dtype)


def causal_depthwise_conv(x, w, b):
    c = x.shape[-1]
    y = lax.conv_general_dilated(x, w[:, None, :], window_strides=(1,), padding=[(CONV_WIDTH - 1, 0)],
                                 dimension_numbers=("NWC", "WIO", "NWC"), feature_group_count=c)
    return y + b


def rg_lru(x, w_a, b_a, w_x, b_x, a_param):
    bsz, s, _ = x.shape
    xf = x.astype(jnp.float32)
    xh = xf.reshape(bsz, s, LRU_HEADS, LRU_HEAD_DIM)
    r = jax.nn.sigmoid(jnp.einsum("bshi,hij->bshj", xh, w_a.astype(jnp.float32)) + b_a.astype(jnp.float32))
    i = jax.nn.sigmoid(jnp.einsum("bshi,hij->bshj", xh, w_x.astype(jnp.float32)) + b_x.astype(jnp.float32))
    r = r.reshape(bsz, s, LRU_WIDTH)
    i = i.reshape(bsz, s, LRU_WIDTH)
    log_a = -LRU_C * r * jax.nn.softplus(-a_param.astype(jnp.float32))
    a = jnp.exp(log_a)
    mult = jnp.sqrt(-jnp.expm1(2.0 * log_a))
    first = (jnp.arange(s) == 0)[None, :, None]
    mult = jnp.where(first, 1.0, mult)
    u = xf * i * mult

    def combine(left, right):
        a_l, u_l = left
        a_r, u_r = right
        return a_l * a_r, a_r * u_l + u_r

    _, h = lax.associative_scan(combine, (a, u), axis=1)
    return h


def ssd_chunked(x, dt, A, Bm, Cm):
    bsz, s, nh, hp = x.shape
    g, n = Bm.shape[2], Bm.shape[3]
    j = nh // g
    c = s // SSD_CHUNK
    l = SSD_CHUNK
    xdt = (x * dt[..., None]).reshape(bsz, c, l, g, j, hp)
    a = (dt * A).reshape(bsz, c, l, g, j).transpose(0, 3, 4, 1, 2)
    Bc = Bm.reshape(bsz, c, l, g, n)
    Cc = Cm.reshape(bsz, c, l, g, n)
    a_cs = jnp.cumsum(a, axis=-1)
    causal = jnp.tril(jnp.ones((l, l), dtype=bool))
    seg = a_cs[..., :, None] - a_cs[..., None, :]
    decay_mat = jnp.exp(jnp.where(causal, seg, -jnp.inf))
    cb = jnp.einsum("bclgn,bcsgn->bgcls", Cc, Bc)
    scores = cb[:, :, None] * decay_mat
    y_diag = jnp.einsum("bgjcls,bcsgjp->bclgjp", scores, xdt)
    decay_states = jnp.exp(a_cs[..., -1:] - a_cs)
    states = jnp.einsum("bcsgn,bgjcs,bcsgjp->bcgjpn", Bc, decay_states, xdt)
    chunk_decay = jnp.exp(a_cs[..., -1])

    def step(carry, inp):
        st, dec = inp
        new = dec[..., None, None] * carry + st
        return new, carry

    init = jnp.zeros((bsz, g, j, hp, n), jnp.float32)
    _, prev = lax.scan(step, init, (states.transpose(1, 0, 2, 3, 4, 5), chunk_decay.transpose(3, 0, 1, 2)))
    prev = prev.transpose(1, 0, 2, 3, 4, 5)
    y_off = jnp.einsum("bclgn,bcgjpn,bgjcl->bclgjp", Cc, prev, jnp.exp(a_cs))
    return (y_diag + y_off).reshape(bsz, s, nh, hp)


def gated_rmsnorm(y, z, w):
    bsz, s, _ = y.shape
    yf = y * jax.nn.silu(z.astype(jnp.float32))
    yg = yf.reshape(bsz, s, SSD_GROUPS, -1)
    yg = yg * lax.rsqrt(jnp.mean(yg * yg, axis=-1, keepdims=True) + RMS_EPS)
    return yg.reshape(bsz, s, -1) * w.astype(jnp.float32)


def setup_inputs(seed: int = 0) -> dict:
    key = jax.random.key(seed)
    ks = jax.random.split(key, 32)
    f32 = jnp.float32
    L = DEPTH
    nrm = lambda k, shape, scale: jax.random.normal(k, shape, f32) * scale
    x = jax.random.normal(ks[0], (BATCH, SEQ, D_MODEL), f32)
    p = jax.random.normal(ks[1], (DEPTH, BATCH, SEQ, PLE_DIM), f32)
    w_in = nrm(ks[2], (L, D_MODEL, D_IN_PROJ), D_MODEL ** -0.5)
    lru_conv_w = nrm(ks[3], (L, CONV_WIDTH, LRU_WIDTH), CONV_WIDTH ** -0.5)
    lru_conv_b = nrm(ks[4], (L, LRU_WIDTH), 0.02)
    lru_gate_a_w = nrm(ks[5], (L, LRU_HEADS, LRU_HEAD_DIM, LRU_HEAD_DIM), LRU_HEAD_DIM ** -0.5)
    lru_gate_a_b = nrm(ks[6], (L, LRU_HEADS, LRU_HEAD_DIM), 0.02)
    lru_gate_x_w = nrm(ks[7], (L, LRU_HEADS, LRU_HEAD_DIM, LRU_HEAD_DIM), LRU_HEAD_DIM ** -0.5)
    lru_gate_x_b = nrm(ks[8], (L, LRU_HEADS, LRU_HEAD_DIM), 0.02)
    a_pow = jax.random.uniform(ks[9], (L, LRU_WIDTH), f32, 0.9, 0.999) ** (1.0 / LRU_C)
    lru_a_param = jnp.log(a_pow) - jnp.log1p(-a_pow)
    ssd_conv_w = nrm(ks[10], (L, CONV_WIDTH, SSD_XBC), CONV_WIDTH ** -0.5)
    ssd_conv_b = nrm(ks[11], (L, SSD_XBC), 0.02)
    dt0 = jnp.exp(jax.random.uniform(ks[12], (L, SSD_HEADS), f32, math.log(1e-3), math.log(1e-1)))
    ssd_dt_bias = dt0 + jnp.log(-jnp.expm1(-dt0))
    ssd_a_log = jnp.log(jax.random.uniform(ks[13], (L, SSD_HEADS), f32, 1.0, 16.0))
    ssd_d = 1.0 + nrm(ks[14], (L, SSD_HEADS), 0.02)
    ssd_norm_w = 1.0 + nrm(ks[15], (L, SSD_WIDTH), 0.02)
    w_out = nrm(ks[16], (L, D_MIX, D_MODEL), BETA * D_MIX ** -0.5)
    ln1_g = 1.0 + nrm(ks[17], (L, D_MODEL), 0.02)
    ln1_b = nrm(ks[18], (L, D_MODEL), 0.02)
    w_ff1 = nrm(ks[19], (L, D_MODEL, D_FF), D_MODEL ** -0.5)
    w_ff2 = nrm(ks[20], (L, D_FF, D_MODEL), BETA * D_FF ** -0.5)
    ln2_g = 1.0 + nrm(ks[21], (L, D_MODEL), 0.02)
    ln2_b = nrm(ks[22], (L, D_MODEL), 0.02)
    w_ple_gate = nrm(ks[23], (L, D_MODEL, D_MODEL), D_MODEL ** -0.5)
    w_ple = nrm(ks[24], (L, PLE_DIM, D_MODEL), BETA * PLE_DIM ** -0.5)
    ln3_g = 1.0 + nrm(ks[25], (L, D_MODEL), 0.02)
    ln3_b = nrm(ks[26], (L, D_MODEL), 0.02)
    return {"x": x, "p": p, "w_in": w_in, "lru_conv_w": lru_conv_w, "lru_conv_b": lru_conv_b,
            "lru_gate_a_w": lru_gate_a_w, "lru_gate_a_b": lru_gate_a_b, "lru_gate_x_w": lru_gate_x_w,
            "lru_gate_x_b": lru_gate_x_b, "lru_a_param": lru_a_param, "ssd_conv_w": ssd_conv_w,
            "ssd_conv_b": ssd_conv_b, "ssd_dt_bias": ssd_dt_bias, "ssd_a_log": ssd_a_log, "ssd_d": ssd_d,
            "ssd_norm_w": ssd_norm_w, "w_out": w_out, "ln1_g": ln1_g, "ln1_b": ln1_b, "w_ff1": w_ff1,
            "w_ff2": w_ff2, "ln2_g": ln2_g, "ln2_b": ln2_b, "w_ple_gate": w_ple_gate, "w_ple": w_ple,
            "ln3_g": ln3_g, "ln3_b": ln3_b}


def reference(x, p, w_in, lru_conv_w, lru_conv_b, lru_gate_a_w, lru_gate_a_b, lru_gate_x_w, lru_gate_x_b,
              lru_a_param, ssd_conv_w, ssd_conv_b, ssd_dt_bias, ssd_a_log, ssd_d, ssd_norm_w, w_out,
              ln1_g, ln1_b, w_ff1, w_ff2, ln2_g, ln2_b, w_ple_gate, w_ple, ln3_g, ln3_b):
    bsz, s, _ = x.shape
    for i in range(DEPTH):
        proj = jnp.einsum("bsd,de->bse", x, w_in[i])
        x_lru, g_lru, z, xbc, dt_raw = jnp.split(proj, IN_SPLIT_POINTS, axis=-1)
        xl = causal_depthwise_conv(x_lru, lru_conv_w[i], lru_conv_b[i])
        h = rg_lru(xl, lru_gate_a_w[i], lru_gate_a_b[i], lru_gate_x_w[i], lru_gate_x_b[i], lru_a_param[i])
        y_lru = (jax.nn.gelu(g_lru.astype(jnp.float32)) * h).astype(x.dtype)
        xbc = jax.nn.silu(causal_depthwise_conv(xbc, ssd_conv_w[i], ssd_conv_b[i]).astype(jnp.float32))
        xs, Bm, Cm = jnp.split(xbc, (SSD_WIDTH, SSD_WIDTH + SSD_GROUPS * SSD_STATE), axis=-1)
        xs = xs.reshape(bsz, s, SSD_HEADS, SSD_HEAD_DIM)
        Bm = Bm.reshape(bsz, s, SSD_GROUPS, SSD_STATE)
        Cm = Cm.reshape(bsz, s, SSD_GROUPS, SSD_STATE)
        dt = jax.nn.softplus(dt_raw.astype(jnp.float32) + ssd_dt_bias[i].astype(jnp.float32))
        A = -jnp.exp(ssd_a_log[i].astype(jnp.float32))
        y = ssd_chunked(xs, dt, A, Bm, Cm) + xs * ssd_d[i].astype(jnp.float32)[:, None]
        y_ssd = gated_rmsnorm(y.reshape(bsz, s, SSD_WIDTH), z, ssd_norm_w[i]).astype(x.dtype)
        mix = jnp.einsum("bse,ed->bsd", jnp.concatenate([y_lru, y_ssd], axis=-1), w_out[i])
        x = layer_norm(ALPHA * x + mix, ln1_g[i], ln1_b[i])
        hid = jnp.square(jax.nn.relu(jnp.einsum("bsd,df->bsf", x, w_ff1[i])))
        ff = jnp.einsum("bsf,fd->bsd", hid, w_ff2[i])
        x = layer_norm(ALPHA * x + ff, ln2_g[i], ln2_b[i])
        gate = jax.nn.sigmoid(jnp.einsum("bsd,de->bse", x, w_ple_gate[i]).astype(jnp.float32))
        ple = jnp.einsum("bsk,kd->bsd", p[i], w_ple[i]).astype(jnp.float32)
        x = layer_norm(ALPHA * x + (gate * ple).astype(x.dtype), ln3_g[i], ln3_b[i])
    return x
```

```python
import functools
import math

import jax
import jax.numpy as jnp
from jax import lax
from jax.experimental import pallas as pl
from jax.experimental.pallas import tpu as pltpu

F32 = jnp.float32
BF16 = jnp.bfloat16

D_MODEL = 1024
PLE_DIM = 256
LRU_WIDTH = 1024
LRU_HEADS = 16
LRU_HEAD_DIM = 64
LRU_C = 8.0
SSD_WIDTH = 1024
SSD_HEAD_DIM = 64
SSD_HEADS = 16
SSD_GROUPS = 4
SSD_STATE = 128
SSD_CHUNK = 128
SSD_XBC = SSD_WIDTH + 2 * SSD_GROUPS * SSD_STATE
CONV_WIDTH = 4
D_FF = 4 * D_MODEL
DEPTH = 1
ALPHA = (2.0 * DEPTH) ** 0.25
LN_EPS = 1e-5
RMS_EPS = 1e-5

LANES = 128
SUBLANES = 8
GATE_BLOCK = 256
GROUP_WIDTH = SSD_WIDTH // SSD_GROUPS
HEADS_PER_GROUP = SSD_HEADS // SSD_GROUPS
VMEM_LIMIT_BYTES = 56 * 1024 * 1024

OFF_XL = 0
OFF_GL = LRU_WIDTH
OFF_Z = 2 * LRU_WIDTH
OFF_XBC = 2 * LRU_WIDTH + SSD_WIDTH
W_MAIN_COLS = OFF_XBC + SSD_XBC


def _dot(a, b):
    return jnp.dot(a, b, preferred_element_type=F32)


def _sigmoid(x):
    return 1.0 / (1.0 + jnp.exp(-x))


def _silu(x):
    return x * _sigmoid(x)


def _softplus(x):
    return jnp.maximum(x, 0.0) + jnp.log1p(jnp.exp(-jnp.abs(x)))


def _gelu_tanh(x):
    c = math.sqrt(2.0 / math.pi)
    return 0.5 * x * (1.0 + jnp.tanh(c * (x + 0.044715 * (x * x * x))))


def _layer_norm(v, g, b):
    mu = jnp.mean(v, axis=-1, keepdims=True)
    vc = v - mu
    var = jnp.mean(vc * vc, axis=-1, keepdims=True)
    return vc * lax.rsqrt(var + LN_EPS) * g + b


def _causal_conv(cbuf, tail, xin, w_ref, b_ref, ts):
    cbuf[0:SUBLANES, :] = tail[...]
    cbuf[SUBLANES:SUBLANES + ts, :] = xin
    tail[...] = xin[ts - SUBLANES:ts, :]
    acc = xin * w_ref[CONV_WIDTH - 1:CONV_WIDTH, :] + b_ref[...]
    for k in range(CONV_WIDTH - 1):
        start = SUBLANES - (CONV_WIDTH - 1) + k
        acc = acc + cbuf[start:start + ts, :] * w_ref[k:k + 1, :]
    return acc


def _expand_pair(v, h0, lane_lt_half):
    return jnp.where(lane_lt_half, v[:, h0:h0 + 1], v[:, h0 + 1:h0 + 2])


def _mixer_kernel(x_ref, w_ref, wdt_ref, lcw_ref, lcb_ref, wg_ref, ba_ref, bx_ref, ap_ref,
                  scw_ref, scb_ref, dtb_ref, alog_ref, dexp_ref, nw_ref, wout_ref, g1_ref, b1_ref,
                  o_ref,
                  ltail, stail, hcarry, state, lcbuf, scbuf, a_s, u_s, h_s, ymix, ys, *, ts):
    si = pl.program_id(1)

    @pl.when(si == 0)
    def _():
        ltail[...] = jnp.zeros_like(ltail)
        stail[...] = jnp.zeros_like(stail)
        hcarry[...] = jnp.zeros_like(hcarry)
        state[...] = jnp.zeros_like(state)

    x = x_ref[...]
    xb = x.astype(BF16)
    row = lax.broadcasted_iota(jnp.int32, (ts, 1), 0)
    is_first = (row + si * ts) == 0
    sub_idx = lax.broadcasted_iota(jnp.int32, (ts, 1), 0) % SUBLANES

    xl = _dot(xb, w_ref[:, OFF_XL:OFF_XL + LRU_WIDTH])
    xc = _causal_conv(lcbuf, ltail, xl, lcw_ref, lcb_ref, ts)
    xcb = xc.astype(BF16)
    for j in range(LRU_WIDTH // GATE_BLOCK):
        cs = slice(j * GATE_BLOCK, (j + 1) * GATE_BLOCK)
        gr = _dot(xcb[:, cs], wg_ref[j])
        r = _sigmoid(gr[:, :GATE_BLOCK] + ba_ref[:, cs])
        ig = _sigmoid(gr[:, GATE_BLOCK:] + bx_ref[:, cs])
        log_a = (-LRU_C) * r * _softplus(-ap_ref[:, cs])
        a = jnp.exp(log_a)
        mult = jnp.sqrt(1.0 - a * a)
        mult = jnp.where(is_first, 1.0, mult)
        u = xc[:, cs] * ig * mult
        for s in (1, 2, 4):
            valid = sub_idx >= s
            a_sh = jnp.where(valid, pltpu.roll(a, s, 0), 1.0)
            u_sh = jnp.where(valid, pltpu.roll(u, s, 0), 0.0)
            u = a * u_sh + u
            a = a * a_sh
        a_s[:, cs] = a
        u_s[:, cs] = u

    def carry_body(g, h_prev):
        r0 = pl.multiple_of(g * SUBLANES, SUBLANES)
        a_g = a_s[pl.ds(r0, SUBLANES), :]
        u_g = u_s[pl.ds(r0, SUBLANES), :]
        h_g = a_g * h_prev + u_g
        h_s[pl.ds(r0, SUBLANES), :] = h_g
        return jnp.broadcast_to(h_g[SUBLANES - 1:SUBLANES, :], (SUBLANES, LRU_WIDTH))

    h_last = lax.fori_loop(0, ts // SUBLANES, carry_body, hcarry[...], unroll=8)
    hcarry[...] = h_last

    gl = _dot(xb, w_ref[:, OFF_GL:OFF_GL + LRU_WIDTH])
    ymix[:, 0:LRU_WIDTH] = (_gelu_tanh(gl) * h_s[...]).astype(BF16)

    xbc_raw = _dot(xb, w_ref[:, OFF_XBC:OFF_XBC + SSD_XBC])
    xbc = _silu(_causal_conv(scbuf, stail, xbc_raw, scw_ref, scb_ref, ts))
    dt = _softplus(_dot(xb, wdt_ref[...]) + dtb_ref[...])
    a_neg = -jnp.exp(alog_ref[...])
    a_dt = dt * a_neg

    lane = lax.broadcasted_iota(jnp.int32, (SSD_CHUNK, LANES), 1)
    lane_lt_half = lane < SSD_HEAD_DIM
    rows_c = lax.broadcasted_iota(jnp.int32, (SSD_CHUNK, SSD_CHUNK), 0)
    cols_c = lax.broadcasted_iota(jnp.int32, (SSD_CHUNK, SSD_CHUNK), 1)
    causal = rows_c >= cols_c
    tril = causal.astype(F32)

    for c in range(ts // SSD_CHUNK):
        rs = slice(c * SSD_CHUNK, (c + 1) * SSD_CHUNK)
        a_c = a_dt[rs, :]
        dt_c = dt[rs, :]
        acs = jnp.dot(tril, a_c, preferred_element_type=F32, precision=lax.Precision.HIGHEST)
        acs_t = acs.T
        dt_t = dt_c.T
        acs_last = acs[SSD_CHUNK - 1:SSD_CHUNK, :]
        e_acs = jnp.exp(acs)
        w_st = dt_c * jnp.exp(acs_last - acs)
        e_last = jnp.exp(acs_last)
        for g in range(SSD_GROUPS):
            xcol = slice(g * GROUP_WIDTH, (g + 1) * GROUP_WIDTH)
            bcol = slice(SSD_WIDTH + g * SSD_STATE, SSD_WIDTH + (g + 1) * SSD_STATE)
            ccol = slice(SSD_WIDTH + SSD_GROUPS * SSD_STATE + g * SSD_STATE,
                         SSD_WIDTH + SSD_GROUPS * SSD_STATE + (g + 1) * SSD_STATE)
            xs_g = xbc[rs, xcol]
            b_g = xbc[rs, bcol]
            c_g = xbc[rs, ccol]
            b_gb = b_g.astype(BF16)
            c_gb = c_g.astype(BF16)
            xs_gb = xs_g.astype(BF16)
            cb = lax.dot_general(c_gb, b_gb, (((1,), (1,)), ((), ())), preferred_element_type=F32)
            y_halves = []
            e_halves = []
            w_halves = []
            for p in range(HEADS_PER_GROUP // 2):
                h0 = g * HEADS_PER_GROUP + 2 * p
                xs_pair = xs_gb[:, p * LANES:(p + 1) * LANES]
                yd = []
                for h in (h0, h0 + 1):
                    seg = acs[:, h:h + 1] - acs_t[h:h + 1, :]
                    dec = jnp.exp(jnp.where(causal, seg, -jnp.inf))
                    sc = (cb * dec * dt_t[h:h + 1, :]).astype(BF16)
                    yd.append(_dot(sc, xs_pair))
                y_halves.append(jnp.where(lane_lt_half, yd[0], yd[1]))
                e_halves.append(_expand_pair(e_acs, h0, lane_lt_half))
                w_halves.append(_expand_pair(w_st, h0, lane_lt_half))
            y_diag = jnp.concatenate(y_halves, axis=1)
            e_x = jnp.concatenate(e_halves, axis=1)
            w_x = jnp.concatenate(w_halves, axis=1)
            st = state[g]
            y_off = _dot(c_gb, st.astype(BF16)) * e_x
            ys[rs, xcol] = y_diag + y_off + xs_g * dexp_ref[:, xcol]
            xw = (xs_g * w_x).astype(BF16)
            s_new = _dot(b_g.T.astype(BF16), xw)
            h0g = g * HEADS_PER_GROUP
            lane_row = lax.broadcasted_iota(jnp.int32, (1, LANES), 1) < SSD_HEAD_DIM
            cd = jnp.concatenate(
                [jnp.where(lane_row, e_last[:, h0g + 2 * p:h0g + 2 * p + 1],
                           e_last[:, h0g + 2 * p + 1:h0g + 2 * p + 2])
                 for p in range(HEADS_PER_GROUP // 2)], axis=1)
            state[g] = st * cd + s_new

    z = _dot(xb, w_ref[:, OFF_Z:OFF_Z + SSD_WIDTH])
    yf = ys[...] * _silu(z)
    for g in range(SSD_GROUPS):
        xcol = slice(g * GROUP_WIDTH, (g + 1) * GROUP_WIDTH)
        yg = yf[:, xcol]
        ms = jnp.mean(yg * yg, axis=-1, keepdims=True)
        yn = yg * lax.rsqrt(ms + RMS_EPS) * nw_ref[:, xcol]
        ymix[:, LRU_WIDTH + g * GROUP_WIDTH:LRU_WIDTH + (g + 1) * GROUP_WIDTH] = yn.astype(BF16)

    mix = _dot(ymix[...], wout_ref[...])
    o_ref[...] = _layer_norm(ALPHA * x + mix, g1_ref[...], b1_ref[...])


def _channel_kernel(x_ref, p_ref, w1_ref, w2_ref, g2_ref, b2_ref, wgate_ref, wple_ref, g3_ref, b3_ref,
                    o_ref):
    x = x_ref[...]
    hid = _dot(x.astype(BF16), w1_ref[...])
    hid = jnp.maximum(hid, 0.0)
    hid = (hid * hid).astype(BF16)
    ff = _dot(hid, w2_ref[...])
    x2 = _layer_norm(ALPHA * x + ff, g2_ref[...], b2_ref[...])
    gate = _sigmoid(_dot(x2.astype(BF16), wgate_ref[...]))
    ple = _dot(p_ref[...].astype(BF16), wple_ref[...])
    o_ref[...] = _layer_norm(ALPHA * x2 + gate * ple, g3_ref[...], b3_ref[...])


def _const_spec(shape):
    nd = len(shape)
    return pl.BlockSpec(shape, lambda *_: (0,) * nd, pipeline_mode=pl.Buffered(1))


def _block_diag_gates(wa, wx):
    def bd(w):
        w4 = w.reshape(LRU_WIDTH // GATE_BLOCK, GATE_BLOCK // LRU_HEAD_DIM, LRU_HEAD_DIM, LRU_HEAD_DIM)
        eye = jnp.eye(GATE_BLOCK // LRU_HEAD_DIM, dtype=w.dtype)
        out = jnp.einsum("jhio,hk->jhiko", w4, eye)
        return out.reshape(LRU_WIDTH // GATE_BLOCK, GATE_BLOCK, GATE_BLOCK)
    return jnp.concatenate([bd(wa), bd(wx)], axis=-1)


def _mixer_call(x, w_main, w_dt, lcw, lcb, wg, ba, bx, ap, scw, scb, dtb, alog, dexp, nw, wout, g1, b1, *, ts):
    bsz, seq, d = x.shape
    grid = (bsz, seq // ts)
    x_spec = pl.BlockSpec((None, ts, d), lambda b, s: (b, s, 0))
    consts = (w_main, w_dt, lcw, lcb, wg, ba, bx, ap, scw, scb, dtb, alog, dexp, nw, wout, g1, b1)
    in_specs = [x_spec] + [_const_spec(c.shape) for c in consts]
    scratch = [
        pltpu.VMEM((SUBLANES, LRU_WIDTH), F32),
        pltpu.VMEM((SUBLANES, SSD_XBC), F32),
        pltpu.VMEM((SUBLANES, LRU_WIDTH), F32),
        pltpu.VMEM((SSD_GROUPS, SSD_STATE, GROUP_WIDTH), F32),
        pltpu.VMEM((ts + SUBLANES, LRU_WIDTH), F32),
        pltpu.VMEM((ts + SUBLANES, SSD_XBC), F32),
        pltpu.VMEM((ts, LRU_WIDTH), F32),
        pltpu.VMEM((ts, LRU_WIDTH), F32),
        pltpu.VMEM((ts, LRU_WIDTH), F32),
        pltpu.VMEM((ts, LRU_WIDTH + SSD_WIDTH), BF16),
        pltpu.VMEM((ts, SSD_WIDTH), F32),
    ]
    return pl.pallas_call(
        functools.partial(_mixer_kernel, ts=ts),
        grid=grid,
        in_specs=in_specs,
        out_specs=pl.BlockSpec((None, ts, d), lambda b, s: (b, s, 0)),
        out_shape=jax.ShapeDtypeStruct((bsz, seq, d), F32),
        scratch_shapes=scratch,
        compiler_params=pltpu.CompilerParams(
            dimension_semantics=("arbitrary", "arbitrary"),
            vmem_limit_bytes=VMEM_LIMIT_BYTES),
        name="mixer",
    )(x, *consts)


def _channel_call(x1, p, w1, w2, g2, b2, wgate, wple, g3, b3, *, tm):
    n, d = x1.shape
    consts = (w1, w2, g2, b2, wgate, wple, g3, b3)
    in_specs = [pl.BlockSpec((tm, d), lambda i: (i, 0)),
                pl.BlockSpec((tm, PLE_DIM), lambda i: (i, 0))] + [_const_spec(c.shape) for c in consts]
    return pl.pallas_call(
        _channel_kernel,
        grid=(n // tm,),
        in_specs=in_specs,
        out_specs=pl.BlockSpec((tm, d), lambda i: (i, 0)),
        out_shape=jax.ShapeDtypeStruct((n, d), F32),
        compiler_params=pltpu.CompilerParams(
            dimension_semantics=("arbitrary",),
            vmem_limit_bytes=VMEM_LIMIT_BYTES),
        name="channel",
    )(x1, p, *consts)


MIXER_TS = 256
CHANNEL_TM = 512


def kernel(x, p, w_in, lru_conv_w, lru_conv_b, lru_gate_a_w, lru_gate_a_b, lru_gate_x_w, lru_gate_x_b,
           lru_a_param, ssd_conv_w, ssd_conv_b, ssd_dt_bias, ssd_a_log, ssd_d, ssd_norm_w, w_out,
           ln1_g, ln1_b, w_ff1, w_ff2, ln2_g, ln2_b, w_ple_gate, w_ple, ln3_g, ln3_b):
    bsz, seq, d = x.shape
    row = lambda v: v.reshape(1, -1).astype(F32)
    pad_heads = lambda v: jnp.pad(v.reshape(1, -1).astype(F32), ((0, 0), (0, LANES - SSD_HEADS)))
    for i in range(DEPTH):
        w_main = w_in[i][:, :W_MAIN_COLS].astype(BF16)
        w_dt = jnp.pad(w_in[i][:, W_MAIN_COLS:], ((0, 0), (0, LANES - SSD_HEADS))).astype(BF16)
        wg = _block_diag_gates(lru_gate_a_w[i], lru_gate_x_w[i]).astype(BF16)
        dexp = jnp.repeat(ssd_d[i].astype(F32), SSD_HEAD_DIM).reshape(1, -1)
        x1 = _mixer_call(
            x, w_main, w_dt, lru_conv_w[i].astype(F32), row(lru_conv_b[i]), wg,
            row(lru_gate_a_b[i]), row(lru_gate_x_b[i]), row(lru_a_param[i]),
            ssd_conv_w[i].astype(F32), row(ssd_conv_b[i]), pad_heads(ssd_dt_bias[i]), pad_heads(ssd_a_log[i]),
            dexp, row(ssd_norm_w[i]), w_out[i].astype(BF16), row(ln1_g[i]), row(ln1_b[i]), ts=MIXER_TS)
        x = _channel_call(
            x1.reshape(bsz * seq, d), p[i].reshape(bsz * seq, PLE_DIM),
            w_ff1[i].astype(BF16), w_ff2[i].astype(BF16), row(ln2_g[i]), row(ln2_b[i]),
            w_ple_gate[i].astype(BF16), w_ple[i].astype(BF16), row(ln3_g[i]), row(ln3_b[i]),
            tm=CHANNEL_TM).reshape(bsz, seq, d)
    return x
```

```python
import functools
import math

import jax
import jax.numpy as jnp
from jax import lax
from jax.experimental import pallas as pl
from jax.experimental.pallas import tpu as pltpu

F32 = jnp.float32
BF16 = jnp.bfloat16

D_MODEL = 1024
PLE_DIM = 256
LRU_WIDTH = 1024
LRU_HEADS = 16
LRU_HEAD_DIM = 64
LRU_C = 8.0
SSD_WIDTH = 1024
SSD_HEAD_DIM = 64
SSD_HEADS = 16
SSD_GROUPS = 4
SSD_STATE = 128
SSD_CHUNK = 128
SSD_XBC = SSD_WIDTH + 2 * SSD_GROUPS * SSD_STATE
CONV_WIDTH = 4
D_FF = 4 * D_MODEL
DEPTH = 1
ALPHA = (2.0 * DEPTH) ** 0.25
LN_EPS = 1e-5
RMS_EPS = 1e-5

LANES = 128
SUBLANES = 8
GATE_BLOCK = 256
GROUP_WIDTH = SSD_WIDTH // SSD_GROUPS
HEADS_PER_GROUP = SSD_HEADS // SSD_GROUPS
VMEM_LIMIT_BYTES = 56 * 1024 * 1024

OFF_XL = 0
OFF_GL = LRU_WIDTH
OFF_Z = 2 * LRU_WIDTH
OFF_XBC = 2 * LRU_WIDTH + SSD_WIDTH
W_MAIN_COLS = OFF_XBC + SSD_XBC


def _dot(a, b):
    return jnp.dot(a, b, preferred_element_type=F32)


def _sigmoid(x):
    return 1.0 / (1.0 + jnp.exp(-x))


def _silu(x):
    return x * _sigmoid(x)


def _softplus(x):
    return jnp.maximum(x, 0.0) + jnp.log1p(jnp.exp(-jnp.abs(x)))


def _gelu_tanh(x):
    c = math.sqrt(2.0 / math.pi)
    return 0.5 * x * (1.0 + jnp.tanh(c * (x + 0.044715 * (x * x * x))))


def _layer_norm(v, g, b):
    mu = jnp.mean(v, axis=-1, keepdims=True)
    vc = v - mu
    var = jnp.mean(vc * vc, axis=-1, keepdims=True)
    return vc * lax.rsqrt(var + LN_EPS) * g + b


def _causal_conv(cbuf, tail, xin, w_ref, b_ref, ts):
    cbuf[0:SUBLANES, :] = tail[...]
    cbuf[SUBLANES:SUBLANES + ts, :] = xin
    tail[...] = xin[ts - SUBLANES:ts, :]
    acc = xin * w_ref[CONV_WIDTH - 1:CONV_WIDTH, :] + b_ref[...]
    for k in range(CONV_WIDTH - 1):
        start = SUBLANES - (CONV_WIDTH - 1) + k
        acc = acc + cbuf[start:start + ts, :] * w_ref[k:k + 1, :]
    return acc


def _expand_pair(v, h0, lane_lt_half):
    return jnp.where(lane_lt_half, v[:, h0:h0 + 1], v[:, h0 + 1:h0 + 2])


def _mixer_kernel(x_ref, w_ref, wdt_ref, lcw_ref, lcb_ref, wg_ref, ba_ref, bx_ref, ap_ref,
                  scw_ref, scb_ref, dtb_ref, alog_ref, dexp_ref, nw_ref, wout_ref, g1_ref, b1_ref,
                  o_ref,
                  ltail, stail, hcarry, state, lcbuf, scbuf, a_s, u_s, h_s, ymix, ys, *, ts):
    si = pl.program_id(1)

    @pl.when(si == 0)
    def _():
        ltail[...] = jnp.zeros_like(ltail)
        stail[...] = jnp.zeros_like(stail)
        hcarry[...] = jnp.zeros_like(hcarry)
        state[...] = jnp.zeros_like(state)

    x = x_ref[...]
    xb = x.astype(BF16)
    row = lax.broadcasted_iota(jnp.int32, (ts, 1), 0)
    is_first = (row + si * ts) == 0
    sub_idx = lax.broadcasted_iota(jnp.int32, (ts, 1), 0) % SUBLANES

    xl = _dot(xb, w_ref[:, OFF_XL:OFF_XL + LRU_WIDTH])
    xc = _causal_conv(lcbuf, ltail, xl, lcw_ref, lcb_ref, ts)
    xcb = xc.astype(BF16)
    for j in range(LRU_WIDTH // GATE_BLOCK):
        cs = slice(j * GATE_BLOCK, (j + 1) * GATE_BLOCK)
        gr = _dot(xcb[:, cs], wg_ref[j])
        r = _sigmoid(gr[:, :GATE_BLOCK] + ba_ref[:, cs])
        ig = _sigmoid(gr[:, GATE_BLOCK:] + bx_ref[:, cs])
        log_a = (-LRU_C) * r * _softplus(-ap_ref[:, cs])
        a = jnp.exp(log_a)
        one_m_a2 = 1.0 - a * a
        mult = jnp.where(one_m_a2 > 0.0, one_m_a2 * lax.rsqrt(one_m_a2), 0.0)
        mult = jnp.where(is_first, 1.0, mult)
        u = xc[:, cs] * ig * mult
        for s in (1, 2, 4):
            valid = sub_idx >= s
            a_sh = jnp.where(valid, pltpu.roll(a, s, 0), 1.0)
            u_sh = jnp.where(valid, pltpu.roll(u, s, 0), 0.0)
            u = a * u_sh + u
            a = a * a_sh
        a_s[:, cs] = a
        u_s[:, cs] = u

    def carry_body(g, h_prev):
        r0 = pl.multiple_of(g * SUBLANES, SUBLANES)
        a_g = a_s[pl.ds(r0, SUBLANES), :]
        u_g = u_s[pl.ds(r0, SUBLANES), :]
        h_g = a_g * h_prev + u_g
        h_s[pl.ds(r0, SUBLANES), :] = h_g
        return jnp.broadcast_to(h_g[SUBLANES - 1:SUBLANES, :], (SUBLANES, LRU_WIDTH))

    h_last = lax.fori_loop(0, ts // SUBLANES, carry_body, hcarry[...], unroll=8)
    hcarry[...] = h_last

    gl = _dot(xb, w_ref[:, OFF_GL:OFF_GL + LRU_WIDTH])
    ymix[:, 0:LRU_WIDTH] = (_gelu_tanh(gl) * h_s[...]).astype(BF16)

    xbc_raw = _dot(xb, w_ref[:, OFF_XBC:OFF_XBC + SSD_XBC])
    xbc = _silu(_causal_conv(scbuf, stail, xbc_raw, scw_ref, scb_ref, ts))
    dt = _softplus(_dot(xb, wdt_ref[...]) + dtb_ref[...])
    a_neg = -jnp.exp(alog_ref[...])
    a_dt = dt * a_neg

    lane = lax.broadcasted_iota(jnp.int32, (SSD_CHUNK, LANES), 1)
    lane_lt_half = lane < SSD_HEAD_DIM
    rows_c = lax.broadcasted_iota(jnp.int32, (SSD_CHUNK, SSD_CHUNK), 0)
    cols_c = lax.broadcasted_iota(jnp.int32, (SSD_CHUNK, SSD_CHUNK), 1)
    causal = rows_c >= cols_c
    tril = causal.astype(F32)

    for c in range(ts // SSD_CHUNK):
        rs = slice(c * SSD_CHUNK, (c + 1) * SSD_CHUNK)
        a_c = a_dt[rs, :]
        dt_c = dt[rs, :]
        acs = jnp.dot(tril, a_c, preferred_element_type=F32, precision=lax.Precision.HIGHEST)
        acs_t = acs.T
        dt_t = dt_c.T
        acs_last = acs[SSD_CHUNK - 1:SSD_CHUNK, :]
        e_acs = jnp.exp(acs)
        w_st = dt_c * jnp.exp(acs_last - acs)
        e_last = jnp.exp(acs_last)
        for g in range(SSD_GROUPS):
            xcol = slice(g * GROUP_WIDTH, (g + 1) * GROUP_WIDTH)
            bcol = slice(SSD_WIDTH + g * SSD_STATE, SSD_WIDTH + (g + 1) * SSD_STATE)
            ccol = slice(SSD_WIDTH + SSD_GROUPS * SSD_STATE + g * SSD_STATE,
                         SSD_WIDTH + SSD_GROUPS * SSD_STATE + (g + 1) * SSD_STATE)
            xs_g = xbc[rs, xcol]
            b_g = xbc[rs, bcol]
            c_g = xbc[rs, ccol]
            b_gb = b_g.astype(BF16)
            c_gb = c_g.astype(BF16)
            xs_gb = xs_g.astype(BF16)
            cb = lax.dot_general(c_gb, b_gb, (((1,), (1,)), ((), ())), preferred_element_type=F32)
            y_halves = []
            e_halves = []
            w_halves = []
            for p in range(HEADS_PER_GROUP // 2):
                h0 = g * HEADS_PER_GROUP + 2 * p
                xs_pair = xs_gb[:, p * LANES:(p + 1) * LANES]
                yd = []
                for h in (h0, h0 + 1):
                    seg = acs[:, h:h + 1] - acs_t[h:h + 1, :]
                    dec = jnp.exp(jnp.where(causal, seg, -jnp.inf))
                    sc = (cb * dec * dt_t[h:h + 1, :]).astype(BF16)
                    yd.append(_dot(sc, xs_pair))
                y_halves.append(jnp.where(lane_lt_half, yd[0], yd[1]))
                e_halves.append(_expand_pair(e_acs, h0, lane_lt_half))
                w_halves.append(_expand_pair(w_st, h0, lane_lt_half))
            y_diag = jnp.concatenate(y_halves, axis=1)
            e_x = jnp.concatenate(e_halves, axis=1)
            w_x = jnp.concatenate(w_halves, axis=1)
            st = state[g]
            y_off = _dot(c_gb, st.astype(BF16)) * e_x
            ys[rs, xcol] = y_diag + y_off + xs_g * dexp_ref[:, xcol]
            xw = (xs_g * w_x).astype(BF16)
            s_new = _dot(b_g.T.astype(BF16), xw)
            h0g = g * HEADS_PER_GROUP
            lane_row = lax.broadcasted_iota(jnp.int32, (1, LANES), 1) < SSD_HEAD_DIM
            cd = jnp.concatenate(
                [jnp.where(lane_row, e_last[:, h0g + 2 * p:h0g + 2 * p + 1],
                           e_last[:, h0g + 2 * p + 1:h0g + 2 * p + 2])
                 for p in range(HEADS_PER_GROUP // 2)], axis=1)
            state[g] = st * cd + s_new

    z = _dot(xb, w_ref[:, OFF_Z:OFF_Z + SSD_WIDTH])
    yf = ys[...] * _silu(z)
    for g in range(SSD_GROUPS):
        xcol = slice(g * GROUP_WIDTH, (g + 1) * GROUP_WIDTH)
        yg = yf[:, xcol]
        ms = jnp.mean(yg * yg, axis=-1, keepdims=True)
        yn = yg * lax.rsqrt(ms + RMS_EPS) * nw_ref[:, xcol]
        ymix[:, LRU_WIDTH + g * GROUP_WIDTH:LRU_WIDTH + (g + 1) * GROUP_WIDTH] = yn.astype(BF16)

    mix = _dot(ymix[...], wout_ref[...])
    o_ref[...] = _layer_norm(ALPHA * x + mix, g1_ref[...], b1_ref[...])


def _channel_kernel(x_ref, p_ref, w1_ref, w2_ref, g2_ref, b2_ref, wgate_ref, wple_ref, g3_ref, b3_ref,
                    o_ref):
    x = x_ref[...]
    hid = _dot(x.astype(BF16), w1_ref[...])
    hid = jnp.maximum(hid, 0.0)
    hid = (hid * hid).astype(BF16)
    ff = _dot(hid, w2_ref[...])
    x2 = _layer_norm(ALPHA * x + ff, g2_ref[...], b2_ref[...])
    gate = _sigmoid(_dot(x2.astype(BF16), wgate_ref[...]))
    ple = _dot(p_ref[...].astype(BF16), wple_ref[...])
    o_ref[...] = _layer_norm(ALPHA * x2 + gate * ple, g3_ref[...], b3_ref[...])


def _const_spec(shape):
    nd = len(shape)
    return pl.BlockSpec(shape, lambda *_: (0,) * nd, pipeline_mode=pl.Buffered(1))


def _block_diag_gates(wa, wx):
    def bd(w):
        w4 = w.reshape(LRU_WIDTH // GATE_BLOCK, GATE_BLOCK // LRU_HEAD_DIM, LRU_HEAD_DIM, LRU_HEAD_DIM)
        eye = jnp.eye(GATE_BLOCK // LRU_HEAD_DIM, dtype=w.dtype)
        out = jnp.einsum("jhio,hk->jhiko", w4, eye)
        return out.reshape(LRU_WIDTH // GATE_BLOCK, GATE_BLOCK, GATE_BLOCK)
    return jnp.concatenate([bd(wa), bd(wx)], axis=-1)


def _mixer_call(x, w_main, w_dt, lcw, lcb, wg, ba, bx, ap, scw, scb, dtb, alog, dexp, nw, wout, g1, b1, *, ts):
    bsz, seq, d = x.shape
    grid = (bsz, seq // ts)
    x_spec = pl.BlockSpec((None, ts, d), lambda b, s: (b, s, 0))
    consts = (w_main, w_dt, lcw, lcb, wg, ba, bx, ap, scw, scb, dtb, alog, dexp, nw, wout, g1, b1)
    in_specs = [x_spec] + [_const_spec(c.shape) for c in consts]
    scratch = [
        pltpu.VMEM((SUBLANES, LRU_WIDTH), F32),
        pltpu.VMEM((SUBLANES, SSD_XBC), F32),
        pltpu.VMEM((SUBLANES, LRU_WIDTH), F32),
        pltpu.VMEM((SSD_GROUPS, SSD_STATE, GROUP_WIDTH), F32),
        pltpu.VMEM((ts + SUBLANES, LRU_WIDTH), F32),
        pltpu.VMEM((ts + SUBLANES, SSD_XBC), F32),
        pltpu.VMEM((ts, LRU_WIDTH), F32),
        pltpu.VMEM((ts, LRU_WIDTH), F32),
        pltpu.VMEM((ts, LRU_WIDTH), F32),
        pltpu.VMEM((ts, LRU_WIDTH + SSD_WIDTH), BF16),
        pltpu.VMEM((ts, SSD_WIDTH), F32),
    ]
    return pl.pallas_call(
        functools.partial(_mixer_kernel, ts=ts),
        grid=grid,
        in_specs=in_specs,
        out_specs=pl.BlockSpec((None, ts, d), lambda b, s: (b, s, 0)),
        out_shape=jax.ShapeDtypeStruct((bsz, seq, d), F32),
        scratch_shapes=scratch,
        compiler_params=pltpu.CompilerParams(
            dimension_semantics=("arbitrary", "arbitrary"),
            vmem_limit_bytes=VMEM_LIMIT_BYTES),
        name="mixer",
    )(x, *consts)


def _channel_call(x1, p, w1, w2, g2, b2, wgate, wple, g3, b3, *, tm):
    n, d = x1.shape
    consts = (w1, w2, g2, b2, wgate, wple, g3, b3)
    in_specs = [pl.BlockSpec((tm, d), lambda i: (i, 0)),
                pl.BlockSpec((tm, PLE_DIM), lambda i: (i, 0))] + [_const_spec(c.shape) for c in consts]
    return pl.pallas_call(
        _channel_kernel,
        grid=(n // tm,),
        in_specs=in_specs,
        out_specs=pl.BlockSpec((tm, d), lambda i: (i, 0)),
        out_shape=jax.ShapeDtypeStruct((n, d), F32),
        compiler_params=pltpu.CompilerParams(
            dimension_semantics=("arbitrary",),
            vmem_limit_bytes=VMEM_LIMIT_BYTES),
        name="channel",
    )(x1, p, *consts)


MIXER_TS = 512
CHANNEL_TM = 512


def kernel(x, p, w_in, lru_conv_w, lru_conv_b, lru_gate_a_w, lru_gate_a_b, lru_gate_x_w, lru_gate_x_b,
           lru_a_param, ssd_conv_w, ssd_conv_b, ssd_dt_bias, ssd_a_log, ssd_d, ssd_norm_w, w_out,
           ln1_g, ln1_b, w_ff1, w_ff2, ln2_g, ln2_b, w_ple_gate, w_ple, ln3_g, ln3_b):
    bsz, seq, d = x.shape
    row = lambda v: v.reshape(1, -1).astype(F32)
    pad_heads = lambda v: jnp.pad(v.reshape(1, -1).astype(F32), ((0, 0), (0, LANES - SSD_HEADS)))
    for i in range(DEPTH):
        w_main = w_in[i][:, :W_MAIN_COLS].astype(BF16)
        w_dt = jnp.pad(w_in[i][:, W_MAIN_COLS:], ((0, 0), (0, LANES - SSD_HEADS))).astype(BF16)
        wg = _block_diag_gates(lru_gate_a_w[i], lru_gate_x_w[i]).astype(BF16)
        dexp = jnp.repeat(ssd_d[i].astype(F32), SSD_HEAD_DIM).reshape(1, -1)
        x1 = _mixer_call(
            x, w_main, w_dt, lru_conv_w[i].astype(F32), row(lru_conv_b[i]), wg,
            row(lru_gate_a_b[i]), row(lru_gate_x_b[i]), row(lru_a_param[i]),
            ssd_conv_w[i].astype(F32), row(ssd_conv_b[i]), pad_heads(ssd_dt_bias[i]), pad_heads(ssd_a_log[i]),
            dexp, row(ssd_norm_w[i]), w_out[i].astype(BF16), row(ln1_g[i]), row(ln1_b[i]), ts=MIXER_TS)
        x = _channel_call(
            x1.reshape(bsz * seq, d), p[i].reshape(bsz * seq, PLE_DIM),
            w_ff1[i].astype(BF16), w_ff2[i].astype(BF16), row(ln2_g[i]), row(ln2_b[i]),
            w_ple_gate[i].astype(BF16), w_ple[i].astype(BF16), row(ln3_g[i]), row(ln3_b[i]),
            tm=CHANNEL_TM).reshape(bsz, seq, d)
    return x
```

```python
import functools
import math

import jax
import jax.numpy as jnp
from jax import lax
from jax.experimental import pallas as pl
from jax.experimental.pallas import tpu as pltpu

F32 = jnp.float32
BF16 = jnp.bfloat16

D_MODEL = 1024
PLE_DIM = 256
LRU_WIDTH = 1024
LRU_HEADS = 16
LRU_HEAD_DIM = 64
LRU_C = 8.0
SSD_WIDTH = 1024
SSD_HEAD_DIM = 64
SSD_HEADS = 16
SSD_GROUPS = 4
SSD_STATE = 128
SSD_CHUNK = 128
SSD_XBC = SSD_WIDTH + 2 * SSD_GROUPS * SSD_STATE
CONV_WIDTH = 4
D_FF = 4 * D_MODEL
D_MIX = LRU_WIDTH + SSD_WIDTH
DEPTH = 1
ALPHA = (2.0 * DEPTH) ** 0.25
LN_EPS = 1e-5
RMS_EPS = 1e-5

LANES = 128
SUBLANES = 8
GATE_BLOCK = 256
GROUP_WIDTH = SSD_WIDTH // SSD_GROUPS
HEADS_PER_GROUP = SSD_HEADS // SSD_GROUPS
VMEM_LIMIT_BYTES = 58 * 1024 * 1024

CHUNK = SSD_CHUNK
SEG_LEN = CHUNK // SUBLANES
N_SLABS = CHUNK // SUBLANES
HALO = (CONV_WIDTH - 1) * SUBLANES
CBUF_ROWS = HALO + CHUNK

OFF_XL = 0
OFF_GL = LRU_WIDTH
OFF_Z = 2 * LRU_WIDTH
OFF_XBC = 2 * LRU_WIDTH + SSD_WIDTH
W_MAIN_COLS = OFF_XBC + SSD_XBC
D_IN_PROJ = W_MAIN_COLS + SSD_HEADS


def _dot(a, b):
    return jnp.dot(a, b, preferred_element_type=F32)


def _sigmoid(x):
    return 1.0 / (1.0 + jnp.exp(-x))


def _silu(x):
    return x * _sigmoid(x)


def _softplus(x):
    return jnp.maximum(x, 0.0) + jnp.log1p(jnp.exp(-jnp.abs(x)))


def _gelu_tanh(x):
    c = math.sqrt(2.0 / math.pi)
    return 0.5 * x * (1.0 + jnp.tanh(c * (x + 0.044715 * (x * x * x))))


def _layer_norm(v, g, b):
    mu = jnp.mean(v, axis=-1, keepdims=True)
    vc = v - mu
    var = jnp.mean(vc * vc, axis=-1, keepdims=True)
    return vc * lax.rsqrt(var + LN_EPS) * g + b


def _load_cast(w_hbm, dst_fn, stage, sem, n_chunks, rows):
    def copy(c):
        return pltpu.make_async_copy(w_hbm.at[pl.ds(c * rows, rows), :], stage.at[c % 2], sem.at[c % 2])
    copy(0).start()
    for c in range(n_chunks):
        if c + 1 < n_chunks:
            copy(c + 1).start()
        copy(c).wait()
        dst_fn(c, stage[c % 2])


def _store_rows_bf16(dst_ref, rows):
    def fn(c, val):
        dst_ref[c * rows:(c + 1) * rows, :] = val.astype(BF16)
    return fn


def _perm_conv(cbuf, tail, xin, w_ref, b_ref, n_chunks, sub_eq_last):
    outs = []
    for c in range(n_chunks):
        r0 = c * CHUNK
        prev3 = tail[...] if c == 0 else xin[r0 - HALO:r0, :]
        own3 = xin[r0 + CHUNK - HALO:r0 + CHUNK, :]
        for m in range(CONV_WIDTH - 1):
            sl = slice(m * SUBLANES, (m + 1) * SUBLANES)
            merged = jnp.where(sub_eq_last, prev3[sl, :], own3[sl, :])
            cbuf[c, sl, :] = pltpu.roll(merged, 1, 0)
        cbuf[c, HALO:HALO + CHUNK, :] = xin[r0:r0 + CHUNK, :]
        acc = b_ref[...] + cbuf[c, 0:CHUNK, :] * w_ref[0:1, :]
        for k in range(1, CONV_WIDTH):
            acc = acc + cbuf[c, k * SUBLANES:k * SUBLANES + CHUNK, :] * w_ref[k:k + 1, :]
        outs.append(acc)
    tail[...] = xin[n_chunks * CHUNK - HALO:n_chunks * CHUNK, :]
    return outs


def _expand_pair(v, h0, lane_lt_half):
    return jnp.where(lane_lt_half, v[:, h0:h0 + 1], v[:, h0 + 1:h0 + 2])


def _mixer_kernel(x_ref, win_hbm, wout_hbm, lcw_ref, lcb_ref, wg_ref, ba_ref, bx_ref, ap_ref,
                  scw_ref, scb_ref, dtb_ref, alog_ref, dexp_ref, nw_ref, g1_ref, b1_ref,
                  o_ref,
                  w_bf, wdt_bf, wout_bf, ltail, stail, hcarry, state, lcbuf, scbuf,
                  a_s, h_s, xp_s, ymix, ymix_nat, ys, *, ts):
    bi = pl.program_id(0)
    si = pl.program_id(1)
    n_chunks = ts // CHUNK

    @pl.when(jnp.logical_and(bi == 0, si == 0))
    def _():
        in_rows = 64
        out_rows = 256

        def cast_in(stage, sem):
            wdt_bf[...] = jnp.zeros_like(wdt_bf)

            def dst(c, val):
                rs = slice(c * in_rows, (c + 1) * in_rows)
                w_bf[rs, :] = val[:, 0:W_MAIN_COLS].astype(BF16)
                wdt_bf[rs, 0:SSD_HEADS] = val[:, W_MAIN_COLS:D_IN_PROJ].astype(BF16)
            _load_cast(win_hbm, dst, stage, sem, D_MODEL // in_rows, in_rows)

        pl.run_scoped(cast_in, pltpu.VMEM((2, in_rows, D_IN_PROJ), F32), pltpu.SemaphoreType.DMA((2,)))

        def cast_out(stage, sem):
            _load_cast(wout_hbm, _store_rows_bf16(wout_bf, out_rows), stage, sem, D_MIX // out_rows, out_rows)

        pl.run_scoped(cast_out, pltpu.VMEM((2, out_rows, D_MODEL), F32), pltpu.SemaphoreType.DMA((2,)))

    @pl.when(si == 0)
    def _():
        ltail[...] = jnp.zeros_like(ltail)
        stail[...] = jnp.zeros_like(stail)
        hcarry[...] = jnp.zeros_like(hcarry)
        state[...] = jnp.zeros_like(state)

    pr = lax.broadcasted_iota(jnp.int32, (CHUNK, CHUNK), 0)
    pc = lax.broadcasted_iota(jnp.int32, (CHUNK, CHUNK), 1)
    t_of_row = (pr % SUBLANES) * SEG_LEN + pr // SUBLANES
    t_of_col = (pc % SUBLANES) * SEG_LEN + pc // SUBLANES
    perm = (pc == t_of_row).astype(BF16)
    perm_t = (pr == t_of_col).astype(BF16)
    causal = t_of_col <= t_of_row
    tril = causal.astype(F32)

    sub = lax.broadcasted_iota(jnp.int32, (SUBLANES, 1), 0)
    sub_eq_last = sub == SUBLANES - 1
    sub_eq_first = sub == 0

    x = x_ref[...]
    xb = x.astype(BF16)
    for c in range(n_chunks):
        rs = slice(c * CHUNK, (c + 1) * CHUNK)
        xp_s[rs, :] = _dot(perm, xb[rs, :]).astype(BF16)
    xp = xp_s[...]

    row = lax.broadcasted_iota(jnp.int32, (ts, 1), 0)
    is_first = (row + si * ts) == 0

    xl = _dot(xp, w_bf[:, OFF_XL:OFF_XL + LRU_WIDTH])
    xc = jnp.concatenate(_perm_conv(lcbuf, ltail, xl, lcw_ref, lcb_ref, n_chunks, sub_eq_last), axis=0)
    xcb = xc.astype(BF16)
    for jb in range(LRU_WIDTH // GATE_BLOCK):
        cs = slice(jb * GATE_BLOCK, (jb + 1) * GATE_BLOCK)
        gr = _dot(xcb[:, cs], wg_ref[jb])
        r = _sigmoid(gr[:, :GATE_BLOCK] + ba_ref[:, cs])
        ig = _sigmoid(gr[:, GATE_BLOCK:] + bx_ref[:, cs])
        log_a = (-LRU_C) * r * _softplus(-ap_ref[:, cs])
        a = jnp.exp(log_a)
        one_m_a2 = 1.0 - a * a
        mult = jnp.where(one_m_a2 > 0.0, one_m_a2 * lax.rsqrt(one_m_a2), 0.0)
        mult = jnp.where(is_first, 1.0, mult)
        u = xc[:, cs] * ig * mult
        h0 = hcarry[:, cs]
        for c in range(n_chunks):
            r0 = c * CHUNK
            h_run = u[r0:r0 + SUBLANES, :]
            a_run = a[r0:r0 + SUBLANES, :]
            h_s[r0:r0 + SUBLANES, cs] = h_run
            a_s[r0:r0 + SUBLANES, cs] = a_run
            for j in range(1, N_SLABS):
                rj = slice(r0 + j * SUBLANES, r0 + (j + 1) * SUBLANES)
                a_j = a[rj, :]
                h_run = a_j * h_run + u[rj, :]
                a_run = a_j * a_run
                h_s[rj, cs] = h_run
                a_s[rj, cs] = a_run
            a_seg, u_seg = a_run, h_run
            for s in (1, 2, 4):
                valid = sub >= s
                a_sh = jnp.where(valid, pltpu.roll(a_seg, s, 0), 1.0)
                u_sh = jnp.where(valid, pltpu.roll(u_seg, s, 0), 0.0)
                u_seg = a_seg * u_sh + u_seg
                a_seg = a_seg * a_sh
            h_end = a_seg * h0 + u_seg
            h_in = jnp.where(sub_eq_first, h0, pltpu.roll(h_end, 1, 0))
            h0 = jnp.broadcast_to(h_end[SUBLANES - 1:SUBLANES, :], (SUBLANES, GATE_BLOCK))
            for j in range(N_SLABS):
                rj = slice(r0 + j * SUBLANES, r0 + (j + 1) * SUBLANES)
                h_s[rj, cs] = h_s[rj, cs] + a_s[rj, cs] * h_in
        hcarry[:, cs] = h0

    gl = _dot(xp, w_bf[:, OFF_GL:OFF_GL + LRU_WIDTH])
    ymix[:, 0:LRU_WIDTH] = (_gelu_tanh(gl) * h_s[...]).astype(BF16)

    xbc_raw = _dot(xp, w_bf[:, OFF_XBC:OFF_XBC + SSD_XBC])
    xbc_chunks = _perm_conv(scbuf, stail, xbc_raw, scw_ref, scb_ref, n_chunks, sub_eq_last)
    dt = _softplus(_dot(xp, wdt_bf[...]) + dtb_ref[...])
    a_neg = -jnp.exp(alog_ref[...])
    a_dt = dt * a_neg

    lane = lax.broadcasted_iota(jnp.int32, (CHUNK, LANES), 1)
    lane_lt_half = lane < SSD_HEAD_DIM
    lane_row = lax.broadcasted_iota(jnp.int32, (1, LANES), 1) < SSD_HEAD_DIM

    for c in range(n_chunks):
        rs = slice(c * CHUNK, (c + 1) * CHUNK)
        xbc = _silu(xbc_chunks[c])
        a_c = a_dt[rs, :]
        dt_c = dt[rs, :]
        acs = jnp.dot(tril, a_c, preferred_element_type=F32, precision=lax.Precision.HIGHEST)
        acs_t = acs.T
        dt_t = dt_c.T
        acs_last = acs[CHUNK - 1:CHUNK, :]
        e_acs = jnp.exp(acs)
        w_st = dt_c * jnp.exp(acs_last - acs)
        e_last = jnp.exp(acs_last)
        for g in range(SSD_GROUPS):
            xcol = slice(g * GROUP_WIDTH, (g + 1) * GROUP_WIDTH)
            bcol = slice(SSD_WIDTH + g * SSD_STATE, SSD_WIDTH + (g + 1) * SSD_STATE)
            ccol = slice(SSD_WIDTH + SSD_GROUPS * SSD_STATE + g * SSD_STATE,
                         SSD_WIDTH + SSD_GROUPS * SSD_STATE + (g + 1) * SSD_STATE)
            xs_g = xbc[:, xcol]
            b_g = xbc[:, bcol]
            c_g = xbc[:, ccol]
            b_gb = b_g.astype(BF16)
            c_gb = c_g.astype(BF16)
            xs_gb = xs_g.astype(BF16)
            cb = lax.dot_general(c_gb, b_gb, (((1,), (1,)), ((), ())), preferred_element_type=F32)
            y_halves = []
            e_halves = []
            w_halves = []
            for p in range(HEADS_PER_GROUP // 2):
                h0h = g * HEADS_PER_GROUP + 2 * p
                xs_pair = xs_gb[:, p * LANES:(p + 1) * LANES]
                yd = []
                for h in (h0h, h0h + 1):
                    seg = acs[:, h:h + 1] - acs_t[h:h + 1, :]
                    dec = jnp.exp(jnp.where(causal, seg, -jnp.inf))
                    sc = (cb * dec * dt_t[h:h + 1, :]).astype(BF16)
                    yd.append(_dot(sc, xs_pair))
                y_halves.append(jnp.where(lane_lt_half, yd[0], yd[1]))
                e_halves.append(_expand_pair(e_acs, h0h, lane_lt_half))
                w_halves.append(_expand_pair(w_st, h0h, lane_lt_half))
            y_diag = jnp.concatenate(y_halves, axis=1)
            e_x = jnp.concatenate(e_halves, axis=1)
            w_x = jnp.concatenate(w_halves, axis=1)
            st = state[g]
            y_off = _dot(c_gb, st.astype(BF16)) * e_x
            ys[rs, xcol] = y_diag + y_off + xs_g * dexp_ref[:, xcol]
            xw = (xs_g * w_x).astype(BF16)
            s_new = _dot(b_g.T.astype(BF16), xw)
            h0g = g * HEADS_PER_GROUP
            cd = jnp.concatenate(
                [jnp.where(lane_row, e_last[:, h0g + 2 * p:h0g + 2 * p + 1],
                           e_last[:, h0g + 2 * p + 1:h0g + 2 * p + 2])
                 for p in range(HEADS_PER_GROUP // 2)], axis=1)
            state[g] = st * cd + s_new

    z = _dot(xp, w_bf[:, OFF_Z:OFF_Z + SSD_WIDTH])
    yf = ys[...] * _silu(z)
    for g in range(SSD_GROUPS):
        xcol = slice(g * GROUP_WIDTH, (g + 1) * GROUP_WIDTH)
        yg = yf[:, xcol]
        ms = jnp.mean(yg * yg, axis=-1, keepdims=True)
        yn = yg * lax.rsqrt(ms + RMS_EPS) * nw_ref[:, xcol]
        ymix[:, LRU_WIDTH + g * GROUP_WIDTH:LRU_WIDTH + (g + 1) * GROUP_WIDTH] = yn.astype(BF16)

    for c in range(n_chunks):
        rs = slice(c * CHUNK, (c + 1) * CHUNK)
        ymix_nat[rs, :] = _dot(perm_t, ymix[rs, :]).astype(BF16)
    mix = _dot(ymix_nat[...], wout_bf[...])
    o_ref[...] = _layer_norm(ALPHA * x + mix, g1_ref[...], b1_ref[...])


def _channel_kernel(x_ref, p_ref, w1_hbm, w2_hbm, wgate_hbm, wple_hbm, g2_ref, b2_ref, g3_ref, b3_ref,
                    o_ref, w1_bf, w2_bf, wgate_bf, wple_bf):
    @pl.when(pl.program_id(0) == 0)
    def _():
        def cast(w_hbm, dst_ref, rows):
            n_rows, n_cols = w_hbm.shape

            def body(stage, sem):
                _load_cast(w_hbm, _store_rows_bf16(dst_ref, rows), stage, sem, n_rows // rows, rows)
            pl.run_scoped(body, pltpu.VMEM((2, rows, n_cols), F32), pltpu.SemaphoreType.DMA((2,)))

        cast(w1_hbm, w1_bf, 64)
        cast(w2_hbm, w2_bf, 256)
        cast(wgate_hbm, wgate_bf, 256)
        cast(wple_hbm, wple_bf, 128)

    x = x_ref[...]
    hid = _dot(x.astype(BF16), w1_bf[...])
    hid = jnp.maximum(hid, 0.0)
    hid = (hid * hid).astype(BF16)
    ff = _dot(hid, w2_bf[...])
    x2 = _layer_norm(ALPHA * x + ff, g2_ref[...], b2_ref[...])
    gate = _sigmoid(_dot(x2.astype(BF16), wgate_bf[...]))
    ple = _dot(p_ref[...].astype(BF16), wple_bf[...])
    o_ref[...] = _layer_norm(ALPHA * x2 + gate * ple, g3_ref[...], b3_ref[...])


def _const_spec(shape):
    nd = len(shape)
    return pl.BlockSpec(shape, lambda *_: (0,) * nd, pipeline_mode=pl.Buffered(1))


def _block_diag_gates(wa, wx):
    def bd(w):
        w4 = w.reshape(LRU_WIDTH // GATE_BLOCK, GATE_BLOCK // LRU_HEAD_DIM, LRU_HEAD_DIM, LRU_HEAD_DIM)
        eye = jnp.eye(GATE_BLOCK // LRU_HEAD_DIM, dtype=w.dtype)
        out = jnp.einsum("jhio,hk->jhiko", w4, eye)
        return out.reshape(LRU_WIDTH // GATE_BLOCK, GATE_BLOCK, GATE_BLOCK)
    return jnp.concatenate([bd(wa), bd(wx)], axis=-1)


def _mixer_call(x, w_in, w_out, lcw, lcb, wg, ba, bx, ap, scw, scb, dtb, alog, dexp, nw, g1, b1, *, ts):
    bsz, seq, d = x.shape
    grid = (bsz, seq // ts)
    n_chunks = ts // CHUNK
    x_spec = pl.BlockSpec((None, ts, d), lambda b, s: (b, s, 0))
    any_spec = pl.BlockSpec(memory_space=pl.ANY)
    consts = (lcw, lcb, wg, ba, bx, ap, scw, scb, dtb, alog, dexp, nw, g1, b1)
    in_specs = [x_spec, any_spec, any_spec] + [_const_spec(c.shape) for c in consts]
    scratch = [
        pltpu.VMEM((D_MODEL, W_MAIN_COLS), BF16),
        pltpu.VMEM((D_MODEL, LANES), BF16),
        pltpu.VMEM((D_MIX, D_MODEL), BF16),
        pltpu.VMEM((HALO, LRU_WIDTH), F32),
        pltpu.VMEM((HALO, SSD_XBC), F32),
        pltpu.VMEM((SUBLANES, LRU_WIDTH), F32),
        pltpu.VMEM((SSD_GROUPS, SSD_STATE, GROUP_WIDTH), F32),
        pltpu.VMEM((n_chunks, CBUF_ROWS, LRU_WIDTH), F32),
        pltpu.VMEM((n_chunks, CBUF_ROWS, SSD_XBC), F32),
        pltpu.VMEM((ts, LRU_WIDTH), F32),
        pltpu.VMEM((ts, LRU_WIDTH), F32),
        pltpu.VMEM((ts, D_MODEL), BF16),
        pltpu.VMEM((ts, D_MIX), BF16),
        pltpu.VMEM((ts, D_MIX), BF16),
        pltpu.VMEM((ts, SSD_WIDTH), F32),
    ]
    return pl.pallas_call(
        functools.partial(_mixer_kernel, ts=ts),
        grid=grid,
        in_specs=in_specs,
        out_specs=pl.BlockSpec((None, ts, d), lambda b, s: (b, s, 0)),
        out_shape=jax.ShapeDtypeStruct((bsz, seq, d), F32),
        scratch_shapes=scratch,
        compiler_params=pltpu.CompilerParams(
            dimension_semantics=("arbitrary", "arbitrary"),
            vmem_limit_bytes=VMEM_LIMIT_BYTES),
        name="mixer",
    )(x, w_in, w_out, *consts)


def _channel_call(x1, p, w1, w2, wgate, wple, g2, b2, g3, b3, *, tm):
    n, d = x1.shape
    any_spec = pl.BlockSpec(memory_space=pl.ANY)
    consts = (g2, b2, g3, b3)
    in_specs = [pl.BlockSpec((tm, d), lambda i: (i, 0)),
                pl.BlockSpec((tm, PLE_DIM), lambda i: (i, 0)),
                any_spec, any_spec, any_spec, any_spec] + [_const_spec(c.shape) for c in consts]
    scratch = [
        pltpu.VMEM((D_MODEL, D_FF), BF16),
        pltpu.VMEM((D_FF, D_MODEL), BF16),
        pltpu.VMEM((D_MODEL, D_MODEL), BF16),
        pltpu.VMEM((PLE_DIM, D_MODEL), BF16),
    ]
    return pl.pallas_call(
        _channel_kernel,
        grid=(n // tm,),
        in_specs=in_specs,
        out_specs=pl.BlockSpec((tm, d), lambda i: (i, 0)),
        out_shape=jax.ShapeDtypeStruct((n, d), F32),
        scratch_shapes=scratch,
        compiler_params=pltpu.CompilerParams(
            dimension_semantics=("arbitrary",),
            vmem_limit_bytes=VMEM_LIMIT_BYTES),
        name="channel",
    )(x1, p, w1, w2, wgate, wple, *consts)


MIXER_TS = 512
CHANNEL_TM = 512


def kernel(x, p, w_in, lru_conv_w, lru_conv_b, lru_gate_a_w, lru_gate_a_b, lru_gate_x_w, lru_gate_x_b,
           lru_a_param, ssd_conv_w, ssd_conv_b, ssd_dt_bias, ssd_a_log, ssd_d, ssd_norm_w, w_out,
           ln1_g, ln1_b, w_ff1, w_ff2, ln2_g, ln2_b, w_ple_gate, w_ple, ln3_g, ln3_b):
    bsz, seq, d = x.shape
    row = lambda v: v.reshape(1, -1).astype(F32)
    pad_heads = lambda v: jnp.pad(v.reshape(1, -1).astype(F32), ((0, 0), (0, LANES - SSD_HEADS)))
    for i in range(DEPTH):
        wg = _block_diag_gates(lru_gate_a_w[i], lru_gate_x_w[i]).astype(BF16)
        dexp = jnp.repeat(ssd_d[i].astype(F32), SSD_HEAD_DIM).reshape(1, -1)
        x1 = _mixer_call(
            x, w_in[i], w_out[i], lru_conv_w[i].astype(F32), row(lru_conv_b[i]), wg,
            row(lru_gate_a_b[i]), row(lru_gate_x_b[i]), row(lru_a_param[i]),
            ssd_conv_w[i].astype(F32), row(ssd_conv_b[i]), pad_heads(ssd_dt_bias[i]), pad_heads(ssd_a_log[i]),
            dexp, row(ssd_norm_w[i]), row(ln1_g[i]), row(ln1_b[i]), ts=MIXER_TS)
        x = _channel_call(
            x1.reshape(bsz * seq, d), p[i].reshape(bsz * seq, PLE_DIM),
            w_ff1[i], w_ff2[i], w_ple_gate[i], w_ple[i],
            row(ln2_g[i]), row(ln2_b[i]), row(ln3_g[i]), row(ln3_b[i]),
            tm=CHANNEL_TM).reshape(bsz, seq, d)
    return x
```

```python
import functools
import math

import jax
import jax.numpy as jnp
from jax import lax
from jax.experimental import pallas as pl
from jax.experimental.pallas import tpu as pltpu

F32 = jnp.float32
BF16 = jnp.bfloat16

D_MODEL = 1024
PLE_DIM = 256
LRU_WIDTH = 1024
LRU_HEADS = 16
LRU_HEAD_DIM = 64
LRU_C = 8.0
SSD_WIDTH = 1024
SSD_HEAD_DIM = 64
SSD_HEADS = 16
SSD_GROUPS = 4
SSD_STATE = 128
SSD_CHUNK = 128
SSD_XBC = SSD_WIDTH + 2 * SSD_GROUPS * SSD_STATE
CONV_WIDTH = 4
D_FF = 4 * D_MODEL
D_MIX = LRU_WIDTH + SSD_WIDTH
DEPTH = 1
ALPHA = (2.0 * DEPTH) ** 0.25
LN_EPS = 1e-5
RMS_EPS = 1e-5

LANES = 128
SUBLANES = 8
GATE_BLOCK = 256
GROUP_WIDTH = SSD_WIDTH // SSD_GROUPS
HEADS_PER_GROUP = SSD_HEADS // SSD_GROUPS
VMEM_LIMIT_BYTES = 58 * 1024 * 1024

CHUNK = SSD_CHUNK
SEG_LEN = CHUNK // SUBLANES
N_SLABS = CHUNK // SUBLANES
HALO = (CONV_WIDTH - 1) * SUBLANES
CBUF_ROWS = HALO + CHUNK

OFF_XL = 0
OFF_GL = LRU_WIDTH
OFF_Z = 2 * LRU_WIDTH
OFF_XBC = 2 * LRU_WIDTH + SSD_WIDTH
W_MAIN_COLS = OFF_XBC + SSD_XBC
D_IN_PROJ = W_MAIN_COLS + SSD_HEADS


def _dot(a, b):
    return jnp.dot(a, b, preferred_element_type=F32)


def _sigmoid(x):
    return 0.5 + 0.5 * jnp.tanh(0.5 * x)


def _silu(x):
    hx = 0.5 * x
    return hx + hx * jnp.tanh(hx)


def _softplus(x):
    return jnp.maximum(x, 0.0) + jnp.log1p(jnp.exp(-jnp.abs(x)))


def _gelu_tanh(x):
    c = math.sqrt(2.0 / math.pi)
    return 0.5 * x * (1.0 + jnp.tanh(c * (x + 0.044715 * (x * x * x))))


def _layer_norm(v, g, b):
    mu = jnp.mean(v, axis=-1, keepdims=True)
    vc = v - mu
    var = jnp.mean(vc * vc, axis=-1, keepdims=True)
    return vc * lax.rsqrt(var + LN_EPS) * g + b


def _masked_prefix_sum(mask_bf, a):
    hi = a.astype(BF16)
    rest = a - hi.astype(F32)
    mid = rest.astype(BF16)
    lo = (rest - mid.astype(F32)).astype(BF16)
    return _dot(mask_bf, hi) + _dot(mask_bf, mid) + _dot(mask_bf, lo)


def _load_cast(w_hbm, dst_fn, stage, sem, n_chunks, rows):
    def copy(c):
        return pltpu.make_async_copy(w_hbm.at[pl.ds(c * rows, rows), :], stage.at[c % 2], sem.at[c % 2])
    copy(0).start()
    for c in range(n_chunks):
        if c + 1 < n_chunks:
            copy(c + 1).start()
        copy(c).wait()
        dst_fn(c, stage[c % 2])


def _store_rows_bf16(dst_ref, rows):
    def fn(c, val):
        dst_ref[c * rows:(c + 1) * rows, :] = val.astype(BF16)
    return fn


def _perm_conv(pbuf, tail_val, tail, w_ref, b_ref, n_chunks, sub_eq_last, cols):
    outs = []
    for c in range(n_chunks):
        prev3 = tail_val[:, cols] if c == 0 else pbuf[c - 1, CHUNK:CHUNK + HALO, cols]
        own3 = pbuf[c, CHUNK:CHUNK + HALO, cols]
        for m in range(CONV_WIDTH - 1):
            sl = slice(m * SUBLANES, (m + 1) * SUBLANES)
            merged = jnp.where(sub_eq_last, prev3[sl, :], own3[sl, :])
            pbuf[c, sl, cols] = pltpu.roll(merged, 1, 0)
        acc = b_ref[:, cols] + pbuf[c, 0:CHUNK, cols] * w_ref[0:1, cols]
        for k in range(1, CONV_WIDTH):
            acc = acc + pbuf[c, k * SUBLANES:k * SUBLANES + CHUNK, cols] * w_ref[k:k + 1, cols]
        outs.append(acc)
    tail[:, cols] = pbuf[n_chunks - 1, CHUNK:CHUNK + HALO, cols]
    return outs


def _expand_pair(v, h0, lane_lt_half):
    return jnp.where(lane_lt_half, v[:, h0:h0 + 1], v[:, h0 + 1:h0 + 2])


A_PIECE_COLS = 512


def _stage_a_pieces(x, perm, w_bf, wdt_bf, xp_s, pset, n_chunks):
    pxl, pxbc, pg, pz, pdt = pset

    def permute():
        xb = x.astype(BF16)
        for c in range(n_chunks):
            rs = slice(c * CHUNK, (c + 1) * CHUNK)
            xp_s[rs, :] = _dot(perm, xb[rs, :]).astype(BF16)

    def project(col0):
        def run():
            res = _dot(xp_s[...], w_bf[:, col0:col0 + A_PIECE_COLS])
            if col0 < OFF_GL:
                for c in range(n_chunks):
                    pxl[c, HALO:HALO + CHUNK, col0:col0 + A_PIECE_COLS] = res[c * CHUNK:(c + 1) * CHUNK, :]
            elif col0 < OFF_Z:
                pg[:, col0 - OFF_GL:col0 - OFF_GL + A_PIECE_COLS] = res
            elif col0 < OFF_XBC:
                pz[:, col0 - OFF_Z:col0 - OFF_Z + A_PIECE_COLS] = res
            else:
                for c in range(n_chunks):
                    pxbc[c, HALO:HALO + CHUNK, col0 - OFF_XBC:col0 - OFF_XBC + A_PIECE_COLS] = (
                        res[c * CHUNK:(c + 1) * CHUNK, :])
        return run

    def project_dt():
        pdt[...] = _dot(xp_s[...], wdt_bf[...])

    def cols(off, width):
        return [project(c0) for c0 in range(off, off + width, A_PIECE_COLS)]

    return ([permute] + cols(OFF_XL, LRU_WIDTH) + cols(OFF_GL, LRU_WIDTH) + cols(OFF_XBC, SSD_XBC)
            + [project_dt] + cols(OFF_Z, SSD_WIDTH))


def _make_filler(pieces):
    pending = list(pieces)

    def fill(n=None):
        count = len(pending) if n is None else min(n, len(pending))
        for _ in range(count):
            pending.pop(0)()
    return fill


def _stage_b(x, pset, reset, out_ref, out_rows, consts, refs, scratch, masks, n_chunks, fill):
    pxl, pxbc, pg, pz, pdt = pset
    (lcw_ref, lcb_ref, wg_ref, ba_ref, bx_ref, ap_ref, scw_ref, scb_ref, dtb_ref, alog_ref, dexp_ref, nw_ref,
     g1_ref, b1_ref) = consts
    wout_bf, ltail, stail, hcarry, state = refs
    a_s, h_s, ymix, ymix_nat, ys = scratch
    perm_t, causal, tril, sub, sub_eq_last, sub_eq_first, lane_lt_half, lane_row = masks
    ts = n_chunks * CHUNK

    def carried(ref_val):
        return ref_val if reset is None else jnp.where(reset, 0.0, ref_val)

    row = lax.broadcasted_iota(jnp.int32, (ts, 1), 0)
    if reset is None:
        is_first = None
    else:
        is_first = row == jnp.where(reset, 0, -1)

    fill(1 + LRU_WIDTH // A_PIECE_COLS)
    xc = jnp.concatenate(
        _perm_conv(pxl, carried(ltail[...]), ltail, lcw_ref, lcb_ref, n_chunks, sub_eq_last,
                   slice(0, LRU_WIDTH)), axis=0)
    xcb = xc.astype(BF16)
    for jb in range(LRU_WIDTH // GATE_BLOCK):
        cs = slice(jb * GATE_BLOCK, (jb + 1) * GATE_BLOCK)
        gr = _dot(xcb[:, cs], wg_ref[jb])
        r = _sigmoid(gr[:, :GATE_BLOCK] + ba_ref[:, cs])
        ig = _sigmoid(gr[:, GATE_BLOCK:] + bx_ref[:, cs])
        log_a = r * ((-LRU_C) * _softplus(-ap_ref[:, cs]))
        a = jnp.exp(log_a)
        one_m_a2 = 1.0 - a * a
        mult = jnp.where(one_m_a2 > 0.0, one_m_a2 * lax.rsqrt(one_m_a2), 0.0)
        if is_first is not None:
            mult = jnp.where(is_first, 1.0, mult)
        u = xc[:, cs] * ig * mult
        h0 = carried(hcarry[:, cs])
        for c in range(n_chunks):
            r0 = c * CHUNK
            h_run = u[r0:r0 + SUBLANES, :]
            a_run = a[r0:r0 + SUBLANES, :]
            h_s[r0:r0 + SUBLANES, cs] = h_run
            a_s[r0:r0 + SUBLANES, cs] = a_run
            for j in range(1, N_SLABS):
                rj = slice(r0 + j * SUBLANES, r0 + (j + 1) * SUBLANES)
                a_j = a[rj, :]
                h_run = a_j * h_run + u[rj, :]
                a_run = a_j * a_run
                h_s[rj, cs] = h_run
                a_s[rj, cs] = a_run
            a_seg, u_seg = a_run, h_run
            for s in (1, 2, 4):
                valid = sub >= s
                a_sh = jnp.where(valid, pltpu.roll(a_seg, s, 0), 1.0)
                u_sh = jnp.where(valid, pltpu.roll(u_seg, s, 0), 0.0)
                u_seg = a_seg * u_sh + u_seg
                a_seg = a_seg * a_sh
            h_end = a_seg * h0 + u_seg
            h_in = jnp.where(sub_eq_first, h0, pltpu.roll(h_end, 1, 0))
            h0 = jnp.broadcast_to(h_end[SUBLANES - 1:SUBLANES, :], (SUBLANES, GATE_BLOCK))
            for j in range(N_SLABS):
                rj = slice(r0 + j * SUBLANES, r0 + (j + 1) * SUBLANES)
                h_s[rj, cs] = h_s[rj, cs] + a_s[rj, cs] * h_in
        hcarry[:, cs] = h0

    fill(LRU_WIDTH // A_PIECE_COLS)
    ymix[:, 0:LRU_WIDTH] = (_gelu_tanh(pg[...]) * h_s[...]).astype(BF16)

    stail_val = carried(stail[...])
    xbc_blocks = []
    for cb0 in range(0, SSD_XBC, A_PIECE_COLS):
        fill(1)
        blk = _perm_conv(pxbc, stail_val, stail, scw_ref, scb_ref, n_chunks, sub_eq_last,
                         slice(cb0, cb0 + A_PIECE_COLS))
        xbc_blocks.append([_silu(v) for v in blk])
    xbc_chunks = [jnp.concatenate([blk[c] for blk in xbc_blocks], axis=1) for c in range(n_chunks)]
    fill(1)
    dt = _softplus(pdt[...] + dtb_ref[...])
    a_neg = -jnp.exp(alog_ref[...])
    a_dt = dt * a_neg

    for c in range(n_chunks):
        rs = slice(c * CHUNK, (c + 1) * CHUNK)
        xbc = xbc_chunks[c]
        a_c = a_dt[rs, :]
        dt_c = dt[rs, :]
        acs = _masked_prefix_sum(tril, a_c)
        acs_t = acs.T
        dt_t = dt_c.T
        acs_last = acs[CHUNK - 1:CHUNK, :]
        e_acs = jnp.exp(acs)
        w_st = dt_c * jnp.exp(acs_last - acs)
        e_last = jnp.exp(acs_last)
        for g in range(SSD_GROUPS):
            xcol = slice(g * GROUP_WIDTH, (g + 1) * GROUP_WIDTH)
            bcol = slice(SSD_WIDTH + g * SSD_STATE, SSD_WIDTH + (g + 1) * SSD_STATE)
            ccol = slice(SSD_WIDTH + SSD_GROUPS * SSD_STATE + g * SSD_STATE,
                         SSD_WIDTH + SSD_GROUPS * SSD_STATE + (g + 1) * SSD_STATE)
            xs_g = xbc[:, xcol]
            b_g = xbc[:, bcol]
            c_g = xbc[:, ccol]
            b_gb = b_g.astype(BF16)
            c_gb = c_g.astype(BF16)
            xs_gb = xs_g.astype(BF16)
            cb = lax.dot_general(c_gb, b_gb, (((1,), (1,)), ((), ())), preferred_element_type=F32)
            y_halves = []
            e_halves = []
            w_halves = []
            for p in range(HEADS_PER_GROUP // 2):
                h0h = g * HEADS_PER_GROUP + 2 * p
                xs_pair = xs_gb[:, p * LANES:(p + 1) * LANES]
                yd = []
                for h in (h0h, h0h + 1):
                    seg = acs[:, h:h + 1] - acs_t[h:h + 1, :]
                    dec = jnp.exp(jnp.where(causal, seg, -jnp.inf))
                    sc = (cb * dec * dt_t[h:h + 1, :]).astype(BF16)
                    yd.append(_dot(sc, xs_pair))
                y_halves.append(jnp.where(lane_lt_half, yd[0], yd[1]))
                e_halves.append(_expand_pair(e_acs, h0h, lane_lt_half))
                w_halves.append(_expand_pair(w_st, h0h, lane_lt_half))
            y_diag = jnp.concatenate(y_halves, axis=1)
            e_x = jnp.concatenate(e_halves, axis=1)
            w_x = jnp.concatenate(w_halves, axis=1)
            st = state[g]
            if c == 0:
                st = carried(st)
            y_off = _dot(c_gb, st.astype(BF16)) * e_x
            ys[rs, xcol] = y_diag + y_off + xs_g * dexp_ref[:, xcol]
            xw = (xs_g * w_x).astype(BF16)
            s_new = _dot(b_g.T.astype(BF16), xw)
            h0g = g * HEADS_PER_GROUP
            cd = jnp.concatenate(
                [jnp.where(lane_row, e_last[:, h0g + 2 * p:h0g + 2 * p + 1],
                           e_last[:, h0g + 2 * p + 1:h0g + 2 * p + 2])
                 for p in range(HEADS_PER_GROUP // 2)], axis=1)
            state[g] = st * cd + s_new

    fill()
    yf = ys[...] * _silu(pz[...])
    for g in range(SSD_GROUPS):
        xcol = slice(g * GROUP_WIDTH, (g + 1) * GROUP_WIDTH)
        yg = yf[:, xcol]
        ms = jnp.mean(yg * yg, axis=-1, keepdims=True)
        yn = yg * lax.rsqrt(ms + RMS_EPS) * nw_ref[:, xcol]
        ymix[:, LRU_WIDTH + g * GROUP_WIDTH:LRU_WIDTH + (g + 1) * GROUP_WIDTH] = yn.astype(BF16)

    for c in range(n_chunks):
        rs = slice(c * CHUNK, (c + 1) * CHUNK)
        ymix_nat[rs, :] = _dot(perm_t, ymix[rs, :]).astype(BF16)
    mix = _dot(ymix_nat[...], wout_bf[...])
    out_ref[out_rows, :] = _layer_norm(ALPHA * x + mix, g1_ref[...], b1_ref[...])


def _mixer_kernel(x_ref, win_hbm, wout_hbm, *rest, ts, layer, tiles_per_seq):
    consts = rest[:14]
    o_ref = rest[14]
    (w_bf, wdt_bf, wout_bf, ltail, stail, hcarry, state,
     pxl0, pxbc0, pg0, pz0, pdt0, xp0, a_s, h_s, ymix, ymix_nat, ys) = rest[15:]
    pset0 = (pxl0, pxbc0, pg0, pz0, pdt0)
    m = pl.program_id(0)
    n_chunks = ts // CHUNK

    pr = lax.broadcasted_iota(jnp.int32, (CHUNK, CHUNK), 0)
    pc = lax.broadcasted_iota(jnp.int32, (CHUNK, CHUNK), 1)
    t_of_row = (pr % SUBLANES) * SEG_LEN + pr // SUBLANES
    t_of_col = (pc % SUBLANES) * SEG_LEN + pc // SUBLANES
    perm = (pc == t_of_row).astype(BF16)
    perm_t = (pr == t_of_col).astype(BF16)
    causal = t_of_col <= t_of_row
    tril = causal.astype(BF16)
    sub = lax.broadcasted_iota(jnp.int32, (SUBLANES, 1), 0)
    lane = lax.broadcasted_iota(jnp.int32, (CHUNK, LANES), 1)
    masks = (perm_t, causal, tril, sub, sub == SUBLANES - 1, sub == 0, lane < SSD_HEAD_DIM,
             lax.broadcasted_iota(jnp.int32, (1, LANES), 1) < SSD_HEAD_DIM)

    @pl.when(m == 0)
    def _():
        in_rows = 64
        out_rows = 256

        def cast_in(stage, sem):
            wdt_bf[...] = jnp.zeros_like(wdt_bf)

            def dst(c, val):
                rs = slice(c * in_rows, (c + 1) * in_rows)
                w_bf[rs, :] = val[:, 0:W_MAIN_COLS].astype(BF16)
                wdt_bf[rs, 0:SSD_HEADS] = val[:, W_MAIN_COLS:D_IN_PROJ].astype(BF16)
            _load_cast(win_hbm.at[layer], dst, stage, sem, D_MODEL // in_rows, in_rows)

        pl.run_scoped(cast_in, pltpu.VMEM((2, in_rows, D_IN_PROJ), F32), pltpu.SemaphoreType.DMA((2,)))

        def cast_out(stage, sem):
            _load_cast(wout_hbm.at[layer], _store_rows_bf16(wout_bf, out_rows), stage, sem,
                       D_MIX // out_rows, out_rows)

        pl.run_scoped(cast_out, pltpu.VMEM((2, out_rows, D_MODEL), F32), pltpu.SemaphoreType.DMA((2,)))

        ltail[...] = jnp.zeros_like(ltail)
        stail[...] = jnp.zeros_like(stail)
        hcarry[...] = jnp.zeros_like(hcarry)
        state[...] = jnp.zeros_like(state)

    refs = (wout_bf, ltail, stail, hcarry, state)
    scratch = (a_s, h_s, ymix, ymix_nat, ys)
    reset = (m % tiles_per_seq) == 0
    x = x_ref[...]
    fill = _make_filler(_stage_a_pieces(x, perm, w_bf, wdt_bf, xp0, pset0, n_chunks))
    _stage_b(x, pset0, reset, o_ref, slice(0, ts), consts, refs, scratch, masks, n_chunks, fill)


def _channel_kernel(x_ref, p_ref, w1_hbm, w2_hbm, wgate_hbm, wple_hbm, g2_ref, b2_ref, g3_ref, b3_ref,
                    o_ref, w1_bf, w2_bf, wgate_bf, wple_bf, *, layer):
    @pl.when(pl.program_id(0) == 0)
    def _():
        def cast(w_hbm, dst_ref, rows):
            n_rows, n_cols = w_hbm.shape

            def body(stage, sem):
                _load_cast(w_hbm, _store_rows_bf16(dst_ref, rows), stage, sem, n_rows // rows, rows)
            pl.run_scoped(body, pltpu.VMEM((2, rows, n_cols), F32), pltpu.SemaphoreType.DMA((2,)))

        cast(w1_hbm.at[layer], w1_bf, 64)
        cast(w2_hbm.at[layer], w2_bf, 256)
        cast(wgate_hbm.at[layer], wgate_bf, 256)
        cast(wple_hbm.at[layer], wple_bf, 128)

    x = x_ref[...]
    hid = _dot(x.astype(BF16), w1_bf[...])
    hid = jnp.maximum(hid, 0.0)
    hid = (hid * hid).astype(BF16)
    ff = _dot(hid, w2_bf[...])
    x2 = _layer_norm(ALPHA * x + ff, g2_ref[...], b2_ref[...])
    gate = _sigmoid(_dot(x2.astype(BF16), wgate_bf[...]))
    ple = _dot(p_ref[...].astype(BF16), wple_bf[...])
    o_ref[...] = _layer_norm(ALPHA * x2 + gate * ple, g3_ref[...], b3_ref[...])


def _const_spec(shape):
    nd = len(shape)
    return pl.BlockSpec(shape, lambda *_: (0,) * nd, pipeline_mode=pl.Buffered(1))


def _block_diag_gates(wa, wx):
    def bd(w):
        w4 = w.reshape(LRU_WIDTH // GATE_BLOCK, GATE_BLOCK // LRU_HEAD_DIM, LRU_HEAD_DIM, LRU_HEAD_DIM)
        eye = jnp.eye(GATE_BLOCK // LRU_HEAD_DIM, dtype=w.dtype)
        out = jnp.einsum("jhio,hk->jhiko", w4, eye)
        return out.reshape(LRU_WIDTH // GATE_BLOCK, GATE_BLOCK, GATE_BLOCK)
    return jnp.concatenate([bd(wa), bd(wx)], axis=-1)


def _mixer_call(x2d, w_in, w_out, consts, *, ts, seq, layer):
    n_rows, d = x2d.shape
    n_tiles = n_rows // ts
    tiles_per_seq = seq // ts
    n_chunks = ts // CHUNK
    any_spec = pl.BlockSpec(memory_space=pl.ANY)
    in_specs = [pl.BlockSpec((ts, d), lambda m: (m, 0)),
                any_spec, any_spec] + [_const_spec(c.shape) for c in consts]
    handoff = [
        pltpu.VMEM((n_chunks, CBUF_ROWS, LRU_WIDTH), F32),
        pltpu.VMEM((n_chunks, CBUF_ROWS, SSD_XBC), F32),
        pltpu.VMEM((ts, LRU_WIDTH), F32),
        pltpu.VMEM((ts, SSD_WIDTH), F32),
        pltpu.VMEM((ts, LANES), F32),
    ]
    scratch = [
        pltpu.VMEM((D_MODEL, W_MAIN_COLS), BF16),
        pltpu.VMEM((D_MODEL, LANES), BF16),
        pltpu.VMEM((D_MIX, D_MODEL), BF16),
        pltpu.VMEM((HALO, LRU_WIDTH), F32),
        pltpu.VMEM((HALO, SSD_XBC), F32),
        pltpu.VMEM((SUBLANES, LRU_WIDTH), F32),
        pltpu.VMEM((SSD_GROUPS, SSD_STATE, GROUP_WIDTH), F32),
    ] + handoff + [
        pltpu.VMEM((ts, D_MODEL), BF16),
        pltpu.VMEM((ts, LRU_WIDTH), F32),
        pltpu.VMEM((ts, LRU_WIDTH), F32),
        pltpu.VMEM((ts, D_MIX), BF16),
        pltpu.VMEM((ts, D_MIX), BF16),
        pltpu.VMEM((ts, SSD_WIDTH), F32),
    ]
    return pl.pallas_call(
        functools.partial(_mixer_kernel, ts=ts, layer=layer, tiles_per_seq=tiles_per_seq),
        grid=(n_tiles,),
        in_specs=in_specs,
        out_specs=pl.BlockSpec((ts, d), lambda m: (m, 0)),
        out_shape=jax.ShapeDtypeStruct((n_rows, d), F32),
        scratch_shapes=scratch,
        compiler_params=pltpu.CompilerParams(
            dimension_semantics=("arbitrary",),
            vmem_limit_bytes=VMEM_LIMIT_BYTES),
        name="mixer",
    )(x2d, w_in, w_out, *consts)


def _channel_call(x1, p, w1, w2, wgate, wple, g2, b2, g3, b3, *, tm, layer):
    n, d = x1.shape
    any_spec = pl.BlockSpec(memory_space=pl.ANY)
    consts = (g2, b2, g3, b3)
    in_specs = [pl.BlockSpec((tm, d), lambda i: (i, 0)),
                pl.BlockSpec((tm, PLE_DIM), lambda i: (i, 0)),
                any_spec, any_spec, any_spec, any_spec] + [_const_spec(c.shape) for c in consts]
    scratch = [
        pltpu.VMEM((D_MODEL, D_FF), BF16),
        pltpu.VMEM((D_FF, D_MODEL), BF16),
        pltpu.VMEM((D_MODEL, D_MODEL), BF16),
        pltpu.VMEM((PLE_DIM, D_MODEL), BF16),
    ]
    return pl.pallas_call(
        functools.partial(_channel_kernel, layer=layer),
        grid=(n // tm,),
        in_specs=in_specs,
        out_specs=pl.BlockSpec((tm, d), lambda i: (i, 0)),
        out_shape=jax.ShapeDtypeStruct((n, d), F32),
        scratch_shapes=scratch,
        compiler_params=pltpu.CompilerParams(
            dimension_semantics=("arbitrary",),
            vmem_limit_bytes=VMEM_LIMIT_BYTES),
        name="channel",
    )(x1, p, w1, w2, wgate, wple, *consts)


MIXER_TS = 512
CHANNEL_TM = 512


def kernel(x, p, w_in, lru_conv_w, lru_conv_b, lru_gate_a_w, lru_gate_a_b, lru_gate_x_w, lru_gate_x_b,
           lru_a_param, ssd_conv_w, ssd_conv_b, ssd_dt_bias, ssd_a_log, ssd_d, ssd_norm_w, w_out,
           ln1_g, ln1_b, w_ff1, w_ff2, ln2_g, ln2_b, w_ple_gate, w_ple, ln3_g, ln3_b):
    bsz, seq, d = x.shape
    row = lambda v: v.reshape(1, -1).astype(F32)
    pad_heads = lambda v: jnp.pad(v.reshape(1, -1).astype(F32), ((0, 0), (0, LANES - SSD_HEADS)))
    x2d = x.reshape(bsz * seq, d)
    for i in range(DEPTH):
        wg = _block_diag_gates(lru_gate_a_w[i], lru_gate_x_w[i]).astype(BF16)
        dexp = jnp.repeat(ssd_d[i].astype(F32), SSD_HEAD_DIM).reshape(1, -1)
        consts = (lru_conv_w[i].astype(F32), row(lru_conv_b[i]), wg,
                  row(lru_gate_a_b[i]), row(lru_gate_x_b[i]), row(lru_a_param[i]),
                  ssd_conv_w[i].astype(F32), row(ssd_conv_b[i]), pad_heads(ssd_dt_bias[i]),
                  pad_heads(ssd_a_log[i]), dexp, row(ssd_norm_w[i]), row(ln1_g[i]), row(ln1_b[i]))
        x1 = _mixer_call(x2d, w_in, w_out, consts, ts=MIXER_TS, seq=seq, layer=i)
        x2d = _channel_call(
            x1, p[i].reshape(bsz * seq, PLE_DIM), w_ff1, w_ff2, w_ple_gate, w_ple,
            row(ln2_g[i]), row(ln2_b[i]), row(ln3_g[i]), row(ln3_b[i]), tm=CHANNEL_TM, layer=i)
    return x2d.reshape(bsz, seq, d)
```

```python
import functools
import math

import jax
import jax.numpy as jnp
from jax import lax
from jax.experimental import pallas as pl
from jax.experimental.pallas import tpu as pltpu

F32 = jnp.float32
BF16 = jnp.bfloat16

D_MODEL = 1024
PLE_DIM = 256
LRU_WIDTH = 1024
LRU_HEADS = 16
LRU_HEAD_DIM = 64
LRU_C = 8.0
SSD_WIDTH = 1024
SSD_HEAD_DIM = 64
SSD_HEADS = 16
SSD_GROUPS = 4
SSD_STATE = 128
SSD_CHUNK = 128
SSD_XBC = SSD_WIDTH + 2 * SSD_GROUPS * SSD_STATE
CONV_WIDTH = 4
D_FF = 4 * D_MODEL
D_MIX = LRU_WIDTH + SSD_WIDTH
DEPTH = 1
ALPHA = (2.0 * DEPTH) ** 0.25
LN_EPS = 1e-5
RMS_EPS = 1e-5
LOG2E = math.log2(math.e)

LANES = 128
SUBLANES = 8
GATE_BLOCK = 256
GROUP_WIDTH = SSD_WIDTH // SSD_GROUPS
HEADS_PER_GROUP = SSD_HEADS // SSD_GROUPS
VMEM_LIMIT_BYTES = 58 * 1024 * 1024

CHUNK = SSD_CHUNK
SEG_LEN = CHUNK // SUBLANES
N_SLABS = CHUNK // SUBLANES
HALO = (CONV_WIDTH - 1) * SUBLANES
CBUF_ROWS = HALO + CHUNK

OFF_XL = 0
OFF_GL = LRU_WIDTH
OFF_Z = 2 * LRU_WIDTH
OFF_XBC = 2 * LRU_WIDTH + SSD_WIDTH
W_MAIN_COLS = OFF_XBC + SSD_XBC
D_IN_PROJ = W_MAIN_COLS + SSD_HEADS


def _dot(a, b):
    return jnp.dot(a, b, preferred_element_type=F32)


def _sigmoid(x):
    return 0.5 + 0.5 * jnp.tanh(0.5 * x)


def _silu(x):
    hx = 0.5 * x
    return hx + hx * jnp.tanh(hx)


def _softplus(x):
    return jnp.maximum(x, 0.0) + jnp.log1p(jnp.exp(-jnp.abs(x)))


def _gelu_tanh(x):
    c = math.sqrt(2.0 / math.pi)
    return 0.5 * x * (1.0 + jnp.tanh(c * (x + 0.044715 * (x * x * x))))


def _layer_norm(v, g, b):
    mu = jnp.mean(v, axis=-1, keepdims=True)
    vc = v - mu
    var = jnp.mean(vc * vc, axis=-1, keepdims=True)
    return vc * lax.rsqrt(var + LN_EPS) * g + b


def _masked_prefix_sum(mask_bf, a):
    hi = a.astype(BF16)
    rest = a - hi.astype(F32)
    mid = rest.astype(BF16)
    lo = (rest - mid.astype(F32)).astype(BF16)
    return _dot(mask_bf, hi) + _dot(mask_bf, mid) + _dot(mask_bf, lo)


def _load_cast(w_hbm, dst_fn, stage, sem, n_chunks, rows):
    def copy(c):
        return pltpu.make_async_copy(w_hbm.at[pl.ds(c * rows, rows), :], stage.at[c % 2], sem.at[c % 2])
    copy(0).start()
    for c in range(n_chunks):
        if c + 1 < n_chunks:
            copy(c + 1).start()
        copy(c).wait()
        dst_fn(c, stage[c % 2])


def _store_rows_bf16(dst_ref, rows):
    def fn(c, val):
        dst_ref[c * rows:(c + 1) * rows, :] = val.astype(BF16)
    return fn


def _perm_conv(pbuf, tail_val, tail, w_ref, b_ref, n_chunks, sub_eq_last, cols):
    outs = []
    for c in range(n_chunks):
        prev3 = tail_val[:, cols] if c == 0 else pbuf[c - 1, CHUNK:CHUNK + HALO, cols]
        own3 = pbuf[c, CHUNK:CHUNK + HALO, cols]
        for m in range(CONV_WIDTH - 1):
            sl = slice(m * SUBLANES, (m + 1) * SUBLANES)
            merged = jnp.where(sub_eq_last, prev3[sl, :], own3[sl, :])
            pbuf[c, sl, cols] = pltpu.roll(merged, 1, 0)
        acc = b_ref[:, cols] + pbuf[c, 0:CHUNK, cols] * w_ref[0:1, cols]
        for k in range(1, CONV_WIDTH):
            acc = acc + pbuf[c, k * SUBLANES:k * SUBLANES + CHUNK, cols] * w_ref[k:k + 1, cols]
        outs.append(acc)
    tail[:, cols] = pbuf[n_chunks - 1, CHUNK:CHUNK + HALO, cols]
    return outs


def _expand_pair(v, h0, lane_lt_half):
    return jnp.where(lane_lt_half, v[:, h0:h0 + 1], v[:, h0 + 1:h0 + 2])


A_PIECE_COLS = 512
GATES_PER_PIECE = A_PIECE_COLS // GATE_BLOCK


def _stage_a_pieces(x, perm, w_bf, wdt_bf, xp_s, pset, n_chunks):
    pxl, pxbc, pg, pz, pdt = pset

    def permute():
        xb = x.astype(BF16)
        for c in range(n_chunks):
            rs = slice(c * CHUNK, (c + 1) * CHUNK)
            xp_s[rs, :] = _dot(perm, xb[rs, :]).astype(BF16)

    def project(col0):
        def run():
            res = _dot(xp_s[...], w_bf[:, col0:col0 + A_PIECE_COLS])
            if col0 < OFF_GL:
                for c in range(n_chunks):
                    pxl[c, HALO:HALO + CHUNK, col0:col0 + A_PIECE_COLS] = res[c * CHUNK:(c + 1) * CHUNK, :]
            elif col0 < OFF_Z:
                pg[:, col0 - OFF_GL:col0 - OFF_GL + A_PIECE_COLS] = res
            elif col0 < OFF_XBC:
                pz[:, col0 - OFF_Z:col0 - OFF_Z + A_PIECE_COLS] = res
            else:
                for c in range(n_chunks):
                    pxbc[c, HALO:HALO + CHUNK, col0 - OFF_XBC:col0 - OFF_XBC + A_PIECE_COLS] = (
                        res[c * CHUNK:(c + 1) * CHUNK, :])
        return run

    def project_dt():
        pdt[...] = _dot(xp_s[...], wdt_bf[...])

    def cols(off, width):
        return [project(c0) for c0 in range(off, off + width, A_PIECE_COLS)]

    lru = [piece for pair in zip(cols(OFF_XL, LRU_WIDTH), cols(OFF_GL, LRU_WIDTH)) for piece in pair]
    return [permute] + lru + cols(OFF_XBC, SSD_XBC) + [project_dt] + cols(OFF_Z, SSD_WIDTH)


def _make_filler(pieces):
    pending = list(pieces)

    def fill(n=None):
        count = len(pending) if n is None else min(n, len(pending))
        for _ in range(count):
            pending.pop(0)()
    return fill


def _stage_b(x, pset, reset, out_ref, out_rows, consts, refs, scratch, masks, n_chunks, fill):
    pxl, pxbc, pg, pz, pdt = pset
    (lcw_ref, lcb_ref, wg_ref, ba_ref, bx_ref, ap_ref, scw_ref, scb_ref, dtb_ref, alog_ref, dexp_ref, nw_ref,
     g1_ref, b1_ref) = consts
    wout_bf, ltail, stail, hcarry, state = refs
    a_s, h_s, ymix, ymix_nat, ys = scratch
    perm_t, causal, tril, sub, sub_eq_last, sub_eq_first, lane_lt_half, lane_row = masks
    ts = n_chunks * CHUNK

    def carried(ref_val):
        return ref_val if reset is None else jnp.where(reset, 0.0, ref_val)

    row = lax.broadcasted_iota(jnp.int32, (ts, 1), 0)
    if reset is None:
        is_first = None
    else:
        is_first = row == jnp.where(reset, 0, -1)

    fill(1)
    ltail_val = carried(ltail[...])
    for jb in range(LRU_WIDTH // GATE_BLOCK):
        cs = slice(jb * GATE_BLOCK, (jb + 1) * GATE_BLOCK)
        if jb % GATES_PER_PIECE == 0:
            kcols = slice(jb * GATE_BLOCK, jb * GATE_BLOCK + A_PIECE_COLS)
            fill(1)
            xc = jnp.concatenate(
                _perm_conv(pxl, ltail_val, ltail, lcw_ref, lcb_ref, n_chunks, sub_eq_last, kcols), axis=0)
            xcb = xc.astype(BF16)
        ls = slice((jb % GATES_PER_PIECE) * GATE_BLOCK, (jb % GATES_PER_PIECE + 1) * GATE_BLOCK)
        gr = _dot(xcb[:, ls], wg_ref[jb])
        r = _sigmoid(gr[:, :GATE_BLOCK] + ba_ref[:, cs])
        ig = _sigmoid(gr[:, GATE_BLOCK:] + bx_ref[:, cs])
        a = jnp.exp2(r * ((-LRU_C * LOG2E) * _softplus(-ap_ref[:, cs])))
        one_m_a2 = 1.0 - a * a
        mult = jnp.where(one_m_a2 > 0.0, one_m_a2 * lax.rsqrt(one_m_a2), 0.0)
        if is_first is not None:
            mult = jnp.where(is_first, 1.0, mult)
        u = xc[:, ls] * ig * mult
        h0 = carried(hcarry[:, cs])
        for c in range(n_chunks):
            r0 = c * CHUNK
            h_run = u[r0:r0 + SUBLANES, :]
            a_run = a[r0:r0 + SUBLANES, :]
            h_s[r0:r0 + SUBLANES, cs] = h_run
            a_s[r0:r0 + SUBLANES, cs] = a_run
            for j in range(1, N_SLABS):
                rj = slice(r0 + j * SUBLANES, r0 + (j + 1) * SUBLANES)
                a_j = a[rj, :]
                h_run = a_j * h_run + u[rj, :]
                a_run = a_j * a_run
                h_s[rj, cs] = h_run
                a_s[rj, cs] = a_run
            a_seg, u_seg = a_run, h_run
            for s in (1, 2, 4):
                valid = sub >= s
                a_sh = jnp.where(valid, pltpu.roll(a_seg, s, 0), 1.0)
                u_sh = jnp.where(valid, pltpu.roll(u_seg, s, 0), 0.0)
                u_seg = a_seg * u_sh + u_seg
                a_seg = a_seg * a_sh
            h_end = a_seg * h0 + u_seg
            h_in = jnp.where(sub_eq_first, h0, pltpu.roll(h_end, 1, 0))
            h0 = jnp.broadcast_to(h_end[SUBLANES - 1:SUBLANES, :], (SUBLANES, GATE_BLOCK))
            for j in range(N_SLABS):
                rj = slice(r0 + j * SUBLANES, r0 + (j + 1) * SUBLANES)
                h_s[rj, cs] = h_s[rj, cs] + a_s[rj, cs] * h_in
        hcarry[:, cs] = h0
        if jb % GATES_PER_PIECE == GATES_PER_PIECE - 1:
            fill(1)
            ymix[:, kcols] = (_gelu_tanh(pg[:, kcols]) * h_s[:, kcols]).astype(BF16)

    stail_val = carried(stail[...])
    xbc_blocks = []
    for cb0 in range(0, SSD_XBC, A_PIECE_COLS):
        fill(1)
        blk = _perm_conv(pxbc, stail_val, stail, scw_ref, scb_ref, n_chunks, sub_eq_last,
                         slice(cb0, cb0 + A_PIECE_COLS))
        xbc_blocks.append([_silu(v) for v in blk])
    xbc_chunks = [jnp.concatenate([blk[c] for blk in xbc_blocks], axis=1) for c in range(n_chunks)]
    fill(1)
    dt = _softplus(pdt[...] + dtb_ref[...])
    a_neg = -jnp.exp(alog_ref[...])
    a_dt = dt * a_neg

    for c in range(n_chunks):
        rs = slice(c * CHUNK, (c + 1) * CHUNK)
        xbc = xbc_chunks[c]
        a_c = a_dt[rs, :]
        dt_c = dt[rs, :]
        acs = _masked_prefix_sum(tril, a_c) * LOG2E
        acs_t = acs.T
        dt_t = dt_c.T
        acs_last = acs[CHUNK - 1:CHUNK, :]
        e_acs = jnp.exp2(acs)
        w_st = dt_c * jnp.exp2(acs_last - acs)
        e_last = jnp.exp2(acs_last)
        for g in range(SSD_GROUPS):
            xcol = slice(g * GROUP_WIDTH, (g + 1) * GROUP_WIDTH)
            bcol = slice(SSD_WIDTH + g * SSD_STATE, SSD_WIDTH + (g + 1) * SSD_STATE)
            ccol = slice(SSD_WIDTH + SSD_GROUPS * SSD_STATE + g * SSD_STATE,
                         SSD_WIDTH + SSD_GROUPS * SSD_STATE + (g + 1) * SSD_STATE)
            xs_g = xbc[:, xcol]
            b_g = xbc[:, bcol]
            c_g = xbc[:, ccol]
            b_gb = b_g.astype(BF16)
            c_gb = c_g.astype(BF16)
            xs_gb = xs_g.astype(BF16)
            cb = lax.dot_general(c_gb, b_gb, (((1,), (1,)), ((), ())), preferred_element_type=F32)
            y_halves = []
            e_halves = []
            w_halves = []
            for p in range(HEADS_PER_GROUP // 2):
                h0h = g * HEADS_PER_GROUP + 2 * p
                xs_pair = xs_gb[:, p * LANES:(p + 1) * LANES]
                yd = []
                for h in (h0h, h0h + 1):
                    seg = acs[:, h:h + 1] - acs_t[h:h + 1, :]
                    dec = jnp.exp2(jnp.where(causal, seg, -jnp.inf))
                    sc = (cb * dec * dt_t[h:h + 1, :]).astype(BF16)
                    yd.append(_dot(sc, xs_pair))
                y_halves.append(jnp.where(lane_lt_half, yd[0], yd[1]))
                e_halves.append(_expand_pair(e_acs, h0h, lane_lt_half))
                w_halves.append(_expand_pair(w_st, h0h, lane_lt_half))
            y_diag = jnp.concatenate(y_halves, axis=1)
            e_x = jnp.concatenate(e_halves, axis=1)
            w_x = jnp.concatenate(w_halves, axis=1)
            st = state[g]
            if c == 0:
                st = carried(st)
            y_off = _dot(c_gb, st.astype(BF16)) * e_x
            ys[rs, xcol] = y_diag + y_off + xs_g * dexp_ref[:, xcol]
            xw = (xs_g * w_x).astype(BF16)
            s_new = _dot(b_g.T.astype(BF16), xw)
            h0g = g * HEADS_PER_GROUP
            cd = jnp.concatenate(
                [jnp.where(lane_row, e_last[:, h0g + 2 * p:h0g + 2 * p + 1],
                           e_last[:, h0g + 2 * p + 1:h0g + 2 * p + 2])
                 for p in range(HEADS_PER_GROUP // 2)], axis=1)
            state[g] = st * cd + s_new

    for g in range(SSD_GROUPS):
        xcol = slice(g * GROUP_WIDTH, (g + 1) * GROUP_WIDTH)
        if (g * GROUP_WIDTH) % A_PIECE_COLS == 0:
            fill(1)
        yg = ys[:, xcol] * _silu(pz[:, xcol])
        ms = jnp.mean(yg * yg, axis=-1, keepdims=True)
        yn = yg * lax.rsqrt(ms + RMS_EPS) * nw_ref[:, xcol]
        ymix[:, LRU_WIDTH + g * GROUP_WIDTH:LRU_WIDTH + (g + 1) * GROUP_WIDTH] = yn.astype(BF16)

    for c in range(n_chunks):
        rs = slice(c * CHUNK, (c + 1) * CHUNK)
        ymix_nat[rs, :] = _dot(perm_t, ymix[rs, :]).astype(BF16)
    mix = _dot(ymix_nat[...], wout_bf[...])
    out_ref[out_rows, :] = _layer_norm(ALPHA * x + mix, g1_ref[...], b1_ref[...])


def _mixer_kernel(x_ref, win_hbm, wout_hbm, *rest, ts, layer, tiles_per_seq):
    consts = rest[:14]
    o_ref = rest[14]
    (w_bf, wdt_bf, wout_bf, ltail, stail, hcarry, state,
     pxl0, pxbc0, pg0, pz0, pdt0, xp0, a_s, h_s, ymix, ymix_nat, ys) = rest[15:]
    pset0 = (pxl0, pxbc0, pg0, pz0, pdt0)
    m = pl.program_id(0)
    n_chunks = ts // CHUNK

    pr = lax.broadcasted_iota(jnp.int32, (CHUNK, CHUNK), 0)
    pc = lax.broadcasted_iota(jnp.int32, (CHUNK, CHUNK), 1)
    t_of_row = (pr % SUBLANES) * SEG_LEN + pr // SUBLANES
    t_of_col = (pc % SUBLANES) * SEG_LEN + pc // SUBLANES
    perm = (pc == t_of_row).astype(BF16)
    perm_t = (pr == t_of_col).astype(BF16)
    causal = t_of_col <= t_of_row
    tril = causal.astype(BF16)
    sub = lax.broadcasted_iota(jnp.int32, (SUBLANES, 1), 0)
    lane = lax.broadcasted_iota(jnp.int32, (CHUNK, LANES), 1)
    masks = (perm_t, causal, tril, sub, sub == SUBLANES - 1, sub == 0, lane < SSD_HEAD_DIM,
             lax.broadcasted_iota(jnp.int32, (1, LANES), 1) < SSD_HEAD_DIM)

    @pl.when(m == 0)
    def _():
        in_rows = 64
        out_rows = 256

        def cast_in(stage, sem):
            wdt_bf[...] = jnp.zeros_like(wdt_bf)

            def dst(c, val):
                rs = slice(c * in_rows, (c + 1) * in_rows)
                w_bf[rs, :] = val[:, 0:W_MAIN_COLS].astype(BF16)
                wdt_bf[rs, 0:SSD_HEADS] = val[:, W_MAIN_COLS:D_IN_PROJ].astype(BF16)
            _load_cast(win_hbm.at[layer], dst, stage, sem, D_MODEL // in_rows, in_rows)

        pl.run_scoped(cast_in, pltpu.VMEM((2, in_rows, D_IN_PROJ), F32), pltpu.SemaphoreType.DMA((2,)))

        def cast_out(stage, sem):
            _load_cast(wout_hbm.at[layer], _store_rows_bf16(wout_bf, out_rows), stage, sem,
                       D_MIX // out_rows, out_rows)

        pl.run_scoped(cast_out, pltpu.VMEM((2, out_rows, D_MODEL), F32), pltpu.SemaphoreType.DMA((2,)))

        ltail[...] = jnp.zeros_like(ltail)
        stail[...] = jnp.zeros_like(stail)
        hcarry[...] = jnp.zeros_like(hcarry)
        state[...] = jnp.zeros_like(state)

    refs = (wout_bf, ltail, stail, hcarry, state)
    scratch = (a_s, h_s, ymix, ymix_nat, ys)
    reset = (m % tiles_per_seq) == 0
    x = x_ref[...]
    fill = _make_filler(_stage_a_pieces(x, perm, w_bf, wdt_bf, xp0, pset0, n_chunks))
    _stage_b(x, pset0, reset, o_ref, slice(0, ts), consts, refs, scratch, masks, n_chunks, fill)


def _channel_kernel(x_ref, p_ref, w1_hbm, w2_hbm, wgate_hbm, wple_hbm, g2_ref, b2_ref, g3_ref, b3_ref,
                    o_ref, w1_bf, w2_bf, wgate_bf, wple_bf, *, layer):
    @pl.when(pl.program_id(0) == 0)
    def _():
        def cast(w_hbm, dst_ref, rows):
            n_rows, n_cols = w_hbm.shape

            def body(stage, sem):
                _load_cast(w_hbm, _store_rows_bf16(dst_ref, rows), stage, sem, n_rows // rows, rows)
            pl.run_scoped(body, pltpu.VMEM((2, rows, n_cols), F32), pltpu.SemaphoreType.DMA((2,)))

        cast(w1_hbm.at[layer], w1_bf, 64)
        cast(w2_hbm.at[layer], w2_bf, 256)
        cast(wgate_hbm.at[layer], wgate_bf, 256)
        cast(wple_hbm.at[layer], wple_bf, 128)

    x = x_ref[...]
    hid = _dot(x.astype(BF16), w1_bf[...])
    hid = jnp.maximum(hid, 0.0)
    hid = (hid * hid).astype(BF16)
    ff = _dot(hid, w2_bf[...])
    x2 = _layer_norm(ALPHA * x + ff, g2_ref[...], b2_ref[...])
    gate = _sigmoid(_dot(x2.astype(BF16), wgate_bf[...]))
    ple = _dot(p_ref[...].astype(BF16), wple_bf[...])
    o_ref[...] = _layer_norm(ALPHA * x2 + gate * ple, g3_ref[...], b3_ref[...])


def _const_spec(shape):
    nd = len(shape)
    return pl.BlockSpec(shape, lambda *_: (0,) * nd, pipeline_mode=pl.Buffered(1))


def _block_diag_gates(wa, wx):
    def bd(w):
        w4 = w.reshape(LRU_WIDTH // GATE_BLOCK, GATE_BLOCK // LRU_HEAD_DIM, LRU_HEAD_DIM, LRU_HEAD_DIM)
        eye = jnp.eye(GATE_BLOCK // LRU_HEAD_DIM, dtype=w.dtype)
        out = jnp.einsum("jhio,hk->jhiko", w4, eye)
        return out.reshape(LRU_WIDTH // GATE_BLOCK, GATE_BLOCK, GATE_BLOCK)
    return jnp.concatenate([bd(wa), bd(wx)], axis=-1)


def _mixer_call(x2d, w_in, w_out, consts, *, ts, seq, layer):
    n_rows, d = x2d.shape
    n_tiles = n_rows // ts
    tiles_per_seq = seq // ts
    n_chunks = ts // CHUNK
    any_spec = pl.BlockSpec(memory_space=pl.ANY)
    in_specs = [pl.BlockSpec((ts, d), lambda m: (m, 0)),
                any_spec, any_spec] + [_const_spec(c.shape) for c in consts]
    handoff = [
        pltpu.VMEM((n_chunks, CBUF_ROWS, LRU_WIDTH), F32),
        pltpu.VMEM((n_chunks, CBUF_ROWS, SSD_XBC), F32),
        pltpu.VMEM((ts, LRU_WIDTH), F32),
        pltpu.VMEM((ts, SSD_WIDTH), F32),
        pltpu.VMEM((ts, LANES), F32),
    ]
    scratch = [
        pltpu.VMEM((D_MODEL, W_MAIN_COLS), BF16),
        pltpu.VMEM((D_MODEL, LANES), BF16),
        pltpu.VMEM((D_MIX, D_MODEL), BF16),
        pltpu.VMEM((HALO, LRU_WIDTH), F32),
        pltpu.VMEM((HALO, SSD_XBC), F32),
        pltpu.VMEM((SUBLANES, LRU_WIDTH), F32),
        pltpu.VMEM((SSD_GROUPS, SSD_STATE, GROUP_WIDTH), F32),
    ] + handoff + [
        pltpu.VMEM((ts, D_MODEL), BF16),
        pltpu.VMEM((ts, LRU_WIDTH), F32),
        pltpu.VMEM((ts, LRU_WIDTH), F32),
        pltpu.VMEM((ts, D_MIX), BF16),
        pltpu.VMEM((ts, D_MIX), BF16),
        pltpu.VMEM((ts, SSD_WIDTH), F32),
    ]
    return pl.pallas_call(
        functools.partial(_mixer_kernel, ts=ts, layer=layer, tiles_per_seq=tiles_per_seq),
        grid=(n_tiles,),
        in_specs=in_specs,
        out_specs=pl.BlockSpec((ts, d), lambda m: (m, 0)),
        out_shape=jax.ShapeDtypeStruct((n_rows, d), F32),
        scratch_shapes=scratch,
        compiler_params=pltpu.CompilerParams(
            dimension_semantics=("arbitrary",),
            vmem_limit_bytes=VMEM_LIMIT_BYTES),
        name="mixer",
    )(x2d, w_in, w_out, *consts)


def _channel_call(x1, p, w1, w2, wgate, wple, g2, b2, g3, b3, *, tm, layer):
    n, d = x1.shape
    any_spec = pl.BlockSpec(memory_space=pl.ANY)
    consts = (g2, b2, g3, b3)
    in_specs = [pl.BlockSpec((tm, d), lambda i: (i, 0)),
                pl.BlockSpec((None, tm, PLE_DIM), lambda i: (layer, i, 0)),
                any_spec, any_spec, any_spec, any_spec] + [_const_spec(c.shape) for c in consts]
    scratch = [
        pltpu.VMEM((D_MODEL, D_FF), BF16),
        pltpu.VMEM((D_FF, D_MODEL), BF16),
        pltpu.VMEM((D_MODEL, D_MODEL), BF16),
        pltpu.VMEM((PLE_DIM, D_MODEL), BF16),
    ]
    return pl.pallas_call(
        functools.partial(_channel_kernel, layer=layer),
        grid=(n // tm,),
        in_specs=in_specs,
        out_specs=pl.BlockSpec((tm, d), lambda i: (i, 0)),
        out_shape=jax.ShapeDtypeStruct((n, d), F32),
        scratch_shapes=scratch,
        compiler_params=pltpu.CompilerParams(
            dimension_semantics=("arbitrary",),
            vmem_limit_bytes=VMEM_LIMIT_BYTES),
        name="channel",
    )(x1, p, w1, w2, wgate, wple, *consts)


MIXER_TS = 512
CHANNEL_TM = 512


def kernel(x, p, w_in, lru_conv_w, lru_conv_b, lru_gate_a_w, lru_gate_a_b, lru_gate_x_w, lru_gate_x_b,
           lru_a_param, ssd_conv_w, ssd_conv_b, ssd_dt_bias, ssd_a_log, ssd_d, ssd_norm_w, w_out,
           ln1_g, ln1_b, w_ff1, w_ff2, ln2_g, ln2_b, w_ple_gate, w_ple, ln3_g, ln3_b):
    bsz, seq, d = x.shape
    row = lambda v: v.reshape(1, -1).astype(F32)
    pad_heads = lambda v: jnp.pad(v.reshape(1, -1).astype(F32), ((0, 0), (0, LANES - SSD_HEADS)))
    x2d = x.reshape(bsz * seq, d)
    for i in range(DEPTH):
        wg = _block_diag_gates(lru_gate_a_w[i], lru_gate_x_w[i]).astype(BF16)
        dexp = jnp.repeat(ssd_d[i].astype(F32), SSD_HEAD_DIM).reshape(1, -1)
        consts = (lru_conv_w[i].astype(F32), row(lru_conv_b[i]), wg,
                  row(lru_gate_a_b[i]), row(lru_gate_x_b[i]), row(lru_a_param[i]),
                  ssd_conv_w[i].astype(F32), row(ssd_conv_b[i]), pad_heads(ssd_dt_bias[i]),
                  pad_heads(ssd_a_log[i]), dexp, row(ssd_norm_w[i]), row(ln1_g[i]), row(ln1_b[i]))
        x1 = _mixer_call(x2d, w_in, w_out, consts, ts=MIXER_TS, seq=seq, layer=i)
        x2d = _channel_call(
            x1, p.reshape(DEPTH, bsz * seq, PLE_DIM), w_ff1, w_ff2, w_ple_gate, w_ple,
            row(ln2_g[i]), row(ln2_b[i]), row(ln3_g[i]), row(ln3_b[i]), tm=CHANNEL_TM, layer=i)
    return x2d.reshape(bsz, seq, d)
```

```python
import functools
import math

import jax
import jax.numpy as jnp
from jax import lax
from jax.experimental import pallas as pl
from jax.experimental.pallas import tpu as pltpu

F32 = jnp.float32
BF16 = jnp.bfloat16

D_MODEL = 1024
PLE_DIM = 256
LRU_WIDTH = 1024
LRU_HEADS = 16
LRU_HEAD_DIM = 64
LRU_C = 8.0
SSD_WIDTH = 1024
SSD_HEAD_DIM = 64
SSD_HEADS = 16
SSD_GROUPS = 4
SSD_STATE = 128
SSD_CHUNK = 128
SSD_XBC = SSD_WIDTH + 2 * SSD_GROUPS * SSD_STATE
CONV_WIDTH = 4
D_FF = 4 * D_MODEL
D_MIX = LRU_WIDTH + SSD_WIDTH
DEPTH = 1
ALPHA = (2.0 * DEPTH) ** 0.25
LN_EPS = 1e-5
RMS_EPS = 1e-5

LANES = 128
SUBLANES = 8
GATE_BLOCK = 256
GROUP_WIDTH = SSD_WIDTH // SSD_GROUPS
HEADS_PER_GROUP = SSD_HEADS // SSD_GROUPS
VMEM_LIMIT_BYTES = 58 * 1024 * 1024

CHUNK = SSD_CHUNK
SEG_LEN = CHUNK // SUBLANES
N_SLABS = CHUNK // SUBLANES
HALO = (CONV_WIDTH - 1) * SUBLANES
CBUF_ROWS = HALO + CHUNK

OFF_XL = 0
OFF_GL = LRU_WIDTH
OFF_Z = 2 * LRU_WIDTH
OFF_XBC = 2 * LRU_WIDTH + SSD_WIDTH
W_MAIN_COLS = OFF_XBC + SSD_XBC
D_IN_PROJ = W_MAIN_COLS + SSD_HEADS


def _dot(a, b):
    return jnp.dot(a, b, preferred_element_type=F32)


def _dot_nt(a, b_t):
    return lax.dot_general(a, b_t, (((1,), (1,)), ((), ())), preferred_element_type=F32)


def _sigmoid(x):
    return 0.5 + 0.5 * jnp.tanh(0.5 * x)


def _silu(x):
    hx = 0.5 * x
    return hx + hx * jnp.tanh(hx)


def _softplus(x):
    return jnp.maximum(x, 0.0) + jnp.log1p(jnp.exp(-jnp.abs(x)))


def _gelu_tanh(x):
    c = math.sqrt(2.0 / math.pi)
    return 0.5 * x * (1.0 + jnp.tanh(c * (x + 0.044715 * (x * x * x))))


def _layer_norm(v, g, b):
    mu = jnp.mean(v, axis=-1, keepdims=True)
    vc = v - mu
    var = jnp.mean(vc * vc, axis=-1, keepdims=True)
    return vc * lax.rsqrt(var + LN_EPS) * g + b


def _masked_prefix_sum(mask_bf, a):
    hi = a.astype(BF16)
    rest = a - hi.astype(F32)
    mid = rest.astype(BF16)
    lo = (rest - mid.astype(F32)).astype(BF16)
    return _dot(mask_bf, hi) + _dot(mask_bf, mid) + _dot(mask_bf, lo)


def _load_cast(w_hbm, dst_fn, stage, sem, n_chunks, rows):
    def copy(c):
        return pltpu.make_async_copy(w_hbm.at[pl.ds(c * rows, rows), :], stage.at[c % 2], sem.at[c % 2])
    copy(0).start()
    for c in range(n_chunks):
        if c + 1 < n_chunks:
            copy(c + 1).start()
        copy(c).wait()
        dst_fn(c, stage[c % 2])


def _store_rows_bf16(dst_ref, rows):
    def fn(c, val):
        dst_ref[c * rows:(c + 1) * rows, :] = val.astype(BF16)
    return fn


def _perm_conv(pbuf, tail_val, tail, w_ref, b_ref, n_chunks, sub_eq_last, cols):
    outs = []
    for c in range(n_chunks):
        prev3 = tail_val[:, cols] if c == 0 else pbuf[c - 1, CHUNK:CHUNK + HALO, cols]
        own3 = pbuf[c, CHUNK:CHUNK + HALO, cols]
        for m in range(CONV_WIDTH - 1):
            sl = slice(m * SUBLANES, (m + 1) * SUBLANES)
            merged = jnp.where(sub_eq_last, prev3[sl, :], own3[sl, :])
            pbuf[c, sl, cols] = pltpu.roll(merged, 1, 0)
        acc = b_ref[:, cols] + pbuf[c, 0:CHUNK, cols] * w_ref[0:1, cols]
        for k in range(1, CONV_WIDTH):
            acc = acc + pbuf[c, k * SUBLANES:k * SUBLANES + CHUNK, cols] * w_ref[k:k + 1, cols]
        outs.append(acc)
    tail[:, cols] = pbuf[n_chunks - 1, CHUNK:CHUNK + HALO, cols]
    return outs


def _expand_pair(v, h0, lane_lt_half):
    return jnp.where(lane_lt_half, v[:, h0:h0 + 1], v[:, h0 + 1:h0 + 2])


A_PIECE_COLS = 512
GATES_PER_PIECE = A_PIECE_COLS // GATE_BLOCK


def _stage_a_pieces(x, perm, w_bf, wdt_bf, xp_s, pset, n_chunks):
    pxl, pxbc, pg, pz, pdt = pset

    def permute():
        xb = x.astype(BF16)
        for c in range(n_chunks):
            rs = slice(c * CHUNK, (c + 1) * CHUNK)
            xp_s[rs, :] = _dot(perm, xb[rs, :]).astype(BF16)

    def project(col0):
        def run():
            res = _dot_nt(xp_s[...], w_bf[col0:col0 + A_PIECE_COLS, :])
            if col0 < OFF_GL:
                for c in range(n_chunks):
                    pxl[c, HALO:HALO + CHUNK, col0:col0 + A_PIECE_COLS] = res[c * CHUNK:(c + 1) * CHUNK, :]
            elif col0 < OFF_Z:
                pg[:, col0 - OFF_GL:col0 - OFF_GL + A_PIECE_COLS] = res
            elif col0 < OFF_XBC:
                pz[:, col0 - OFF_Z:col0 - OFF_Z + A_PIECE_COLS] = res
            else:
                for c in range(n_chunks):
                    pxbc[c, HALO:HALO + CHUNK, col0 - OFF_XBC:col0 - OFF_XBC + A_PIECE_COLS] = (
                        res[c * CHUNK:(c + 1) * CHUNK, :])
        return run

    def project_dt():
        pdt[...] = _dot_nt(xp_s[...], wdt_bf[...])

    def cols(off, width):
        return [project(c0) for c0 in range(off, off + width, A_PIECE_COLS)]

    lru = [piece for pair in zip(cols(OFF_XL, LRU_WIDTH), cols(OFF_GL, LRU_WIDTH)) for piece in pair]
    return [permute] + lru + cols(OFF_XBC, SSD_XBC) + [project_dt] + cols(OFF_Z, SSD_WIDTH)


def _make_filler(pieces):
    pending = list(pieces)

    def fill(n=None):
        count = len(pending) if n is None else min(n, len(pending))
        for _ in range(count):
            pending.pop(0)()
    return fill


def _stage_b(x, pset, reset, out_ref, out_rows, consts, refs, scratch, masks, n_chunks, fill):
    pxl, pxbc, pg, pz, pdt = pset
    (lcw_ref, lcb_ref, wg_ref, ba_ref, bx_ref, ap_ref, scw_ref, scb_ref, dtb_ref, alog_ref, dexp_ref, nw_ref,
     g1_ref, b1_ref) = consts
    wout_bf, ltail, stail, hcarry, state = refs
    a_s, h_s, ymix, ymix_nat, ys = scratch
    perm_t, causal, tril, sub, sub_eq_last, sub_eq_first, lane_lt_half, lane_row = masks
    ts = n_chunks * CHUNK

    def carried(ref_val):
        return ref_val if reset is None else jnp.where(reset, 0.0, ref_val)

    row = lax.broadcasted_iota(jnp.int32, (ts, 1), 0)
    if reset is None:
        is_first = None
    else:
        is_first = row == jnp.where(reset, 0, -1)

    fill(1)
    ltail_val = carried(ltail[...])
    for jb in range(LRU_WIDTH // GATE_BLOCK):
        cs = slice(jb * GATE_BLOCK, (jb + 1) * GATE_BLOCK)
        if jb % GATES_PER_PIECE == 0:
            kcols = slice(jb * GATE_BLOCK, jb * GATE_BLOCK + A_PIECE_COLS)
            fill(1)
            xc = jnp.concatenate(
                _perm_conv(pxl, ltail_val, ltail, lcw_ref, lcb_ref, n_chunks, sub_eq_last, kcols), axis=0)
            xcb = xc.astype(BF16)
        ls = slice((jb % GATES_PER_PIECE) * GATE_BLOCK, (jb % GATES_PER_PIECE + 1) * GATE_BLOCK)
        gr = _dot(xcb[:, ls], wg_ref[jb])
        r = _sigmoid(gr[:, :GATE_BLOCK] + ba_ref[:, cs])
        ig = _sigmoid(gr[:, GATE_BLOCK:] + bx_ref[:, cs])
        log_a = r * ((-LRU_C) * _softplus(-ap_ref[:, cs]))
        a = jnp.exp(log_a)
        one_m_a2 = 1.0 - a * a
        mult = jnp.where(one_m_a2 > 0.0, one_m_a2 * lax.rsqrt(one_m_a2), 0.0)
        if is_first is not None:
            mult = jnp.where(is_first, 1.0, mult)
        u = xc[:, ls] * ig * mult
        h0 = carried(hcarry[:, cs])
        for c in range(n_chunks):
            r0 = c * CHUNK
            h_run = u[r0:r0 + SUBLANES, :]
            a_run = a[r0:r0 + SUBLANES, :]
            h_s[r0:r0 + SUBLANES, cs] = h_run
            a_s[r0:r0 + SUBLANES, cs] = a_run
            for j in range(1, N_SLABS):
                rj = slice(r0 + j * SUBLANES, r0 + (j + 1) * SUBLANES)
                a_j = a[rj, :]
                h_run = a_j * h_run + u[rj, :]
                a_run = a_j * a_run
                h_s[rj, cs] = h_run
                a_s[rj, cs] = a_run
            a_seg, u_seg = a_run, h_run
            for s in (1, 2, 4):
                valid = sub >= s
                a_sh = jnp.where(valid, pltpu.roll(a_seg, s, 0), 1.0)
                u_sh = jnp.where(valid, pltpu.roll(u_seg, s, 0), 0.0)
                u_seg = a_seg * u_sh + u_seg
                a_seg = a_seg * a_sh
            h_end = a_seg * h0 + u_seg
            h_in = jnp.where(sub_eq_first, h0, pltpu.roll(h_end, 1, 0))
            h0 = jnp.broadcast_to(h_end[SUBLANES - 1:SUBLANES, :], (SUBLANES, GATE_BLOCK))
            for j in range(N_SLABS):
                rj = slice(r0 + j * SUBLANES, r0 + (j + 1) * SUBLANES)
                h_s[rj, cs] = h_s[rj, cs] + a_s[rj, cs] * h_in
        hcarry[:, cs] = h0
        if jb % GATES_PER_PIECE == GATES_PER_PIECE - 1:
            fill(1)
            ymix[:, kcols] = (_gelu_tanh(pg[:, kcols]) * h_s[:, kcols]).astype(BF16)

    stail_val = carried(stail[...])
    xbc_blocks = []
    for cb0 in range(0, SSD_XBC, A_PIECE_COLS):
        fill(1)
        blk = _perm_conv(pxbc, stail_val, stail, scw_ref, scb_ref, n_chunks, sub_eq_last,
                         slice(cb0, cb0 + A_PIECE_COLS))
        xbc_blocks.append([_silu(v) for v in blk])
    xbc_chunks = [jnp.concatenate([blk[c] for blk in xbc_blocks], axis=1) for c in range(n_chunks)]
    fill(1)
    dt = _softplus(pdt[...] + dtb_ref[...])
    a_neg = -jnp.exp(alog_ref[...])
    a_dt = dt * a_neg

    for c in range(n_chunks):
        rs = slice(c * CHUNK, (c + 1) * CHUNK)
        xbc = xbc_chunks[c]
        a_c = a_dt[rs, :]
        dt_c = dt[rs, :]
        acs = _masked_prefix_sum(tril, a_c)
        acs_t = acs.T
        dt_t = dt_c.T
        acs_last = acs[CHUNK - 1:CHUNK, :]
        e_acs = jnp.exp(acs)
        w_st = dt_c * jnp.exp(acs_last - acs)
        e_last = jnp.exp(acs_last)
        for g in range(SSD_GROUPS):
            xcol = slice(g * GROUP_WIDTH, (g + 1) * GROUP_WIDTH)
            bcol = slice(SSD_WIDTH + g * SSD_STATE, SSD_WIDTH + (g + 1) * SSD_STATE)
            ccol = slice(SSD_WIDTH + SSD_GROUPS * SSD_STATE + g * SSD_STATE,
                         SSD_WIDTH + SSD_GROUPS * SSD_STATE + (g + 1) * SSD_STATE)
            xs_g = xbc[:, xcol]
            b_g = xbc[:, bcol]
            c_g = xbc[:, ccol]
            b_gb = b_g.astype(BF16)
            c_gb = c_g.astype(BF16)
            xs_gb = xs_g.astype(BF16)
            cb = _dot_nt(c_gb, b_gb)
            y_halves = []
            e_halves = []
            w_halves = []
            for p in range(HEADS_PER_GROUP // 2):
                h0h = g * HEADS_PER_GROUP + 2 * p
                xs_pair = xs_gb[:, p * LANES:(p + 1) * LANES]
                yd = []
                for h in (h0h, h0h + 1):
                    seg = acs[:, h:h + 1] - acs_t[h:h + 1, :]
                    dec = jnp.exp(jnp.where(causal, seg, -jnp.inf))
                    sc = (cb * dec * dt_t[h:h + 1, :]).astype(BF16)
                    yd.append(_dot(sc, xs_pair))
                y_halves.append(jnp.where(lane_lt_half, yd[0], yd[1]))
                e_halves.append(_expand_pair(e_acs, h0h, lane_lt_half))
                w_halves.append(_expand_pair(w_st, h0h, lane_lt_half))
            y_diag = jnp.concatenate(y_halves, axis=1)
            e_x = jnp.concatenate(e_halves, axis=1)
            w_x = jnp.concatenate(w_halves, axis=1)
            st = state[g]
            if c == 0:
                st = carried(st)
            y_off = _dot(c_gb, st.astype(BF16)) * e_x
            ys[rs, xcol] = y_diag + y_off + xs_g * dexp_ref[:, xcol]
            xw = (xs_g * w_x).astype(BF16)
            s_new = _dot(b_g.T.astype(BF16), xw)
            h0g = g * HEADS_PER_GROUP
            cd = jnp.concatenate(
                [jnp.where(lane_row, e_last[:, h0g + 2 * p:h0g + 2 * p + 1],
                           e_last[:, h0g + 2 * p + 1:h0g + 2 * p + 2])
                 for p in range(HEADS_PER_GROUP // 2)], axis=1)
            state[g] = st * cd + s_new

    for g in range(SSD_GROUPS):
        xcol = slice(g * GROUP_WIDTH, (g + 1) * GROUP_WIDTH)
        if (g * GROUP_WIDTH) % A_PIECE_COLS == 0:
            fill(1)
        yg = ys[:, xcol] * _silu(pz[:, xcol])
        ms = jnp.mean(yg * yg, axis=-1, keepdims=True)
        yn = yg * lax.rsqrt(ms + RMS_EPS) * nw_ref[:, xcol]
        ymix[:, LRU_WIDTH + g * GROUP_WIDTH:LRU_WIDTH + (g + 1) * GROUP_WIDTH] = yn.astype(BF16)

    for c in range(n_chunks):
        rs = slice(c * CHUNK, (c + 1) * CHUNK)
        ymix_nat[rs, :] = _dot(perm_t, ymix[rs, :]).astype(BF16)
    mix = _dot(ymix_nat[...], wout_bf[...])
    out_ref[out_rows, :] = _layer_norm(ALPHA * x + mix, g1_ref[...], b1_ref[...])


def _mixer_kernel(x_ref, win_hbm, wout_hbm, *rest, ts, layer, tiles_per_seq):
    consts = rest[:14]
    o_ref = rest[14]
    (w_bf, wdt_bf, wout_bf, ltail, stail, hcarry, state,
     pxl0, pxbc0, pg0, pz0, pdt0, xp0, a_s, h_s, ymix, ymix_nat, ys) = rest[15:]
    pset0 = (pxl0, pxbc0, pg0, pz0, pdt0)
    m = pl.program_id(0)
    n_chunks = ts // CHUNK

    pr = lax.broadcasted_iota(jnp.int32, (CHUNK, CHUNK), 0)
    pc = lax.broadcasted_iota(jnp.int32, (CHUNK, CHUNK), 1)
    t_of_row = (pr % SUBLANES) * SEG_LEN + pr // SUBLANES
    t_of_col = (pc % SUBLANES) * SEG_LEN + pc // SUBLANES
    perm = (pc == t_of_row).astype(BF16)
    perm_t = (pr == t_of_col).astype(BF16)
    causal = t_of_col <= t_of_row
    tril = causal.astype(BF16)
    sub = lax.broadcasted_iota(jnp.int32, (SUBLANES, 1), 0)
    lane = lax.broadcasted_iota(jnp.int32, (CHUNK, LANES), 1)
    masks = (perm_t, causal, tril, sub, sub == SUBLANES - 1, sub == 0, lane < SSD_HEAD_DIM,
             lax.broadcasted_iota(jnp.int32, (1, LANES), 1) < SSD_HEAD_DIM)

    @pl.when(m == 0)
    def _():
        in_rows = 256
        out_rows = 256

        def cast_in(stage, sem):
            _load_cast(win_hbm.at[layer], _store_rows_bf16(w_bf, in_rows), stage, sem,
                       W_MAIN_COLS // in_rows, in_rows)

        pl.run_scoped(cast_in, pltpu.VMEM((2, in_rows, D_MODEL), F32), pltpu.SemaphoreType.DMA((2,)))

        def cast_dt(stage, sem):
            copy = pltpu.make_async_copy(win_hbm.at[layer, pl.ds(W_MAIN_COLS, SSD_HEADS), :], stage, sem)
            copy.start()
            wdt_bf[...] = jnp.zeros_like(wdt_bf)
            copy.wait()
            wdt_bf[0:SSD_HEADS, :] = stage[...].astype(BF16)

        pl.run_scoped(cast_dt, pltpu.VMEM((SSD_HEADS, D_MODEL), F32), pltpu.SemaphoreType.DMA(()))

        def cast_out(stage, sem):
            _load_cast(wout_hbm.at[layer], _store_rows_bf16(wout_bf, out_rows), stage, sem,
                       D_MIX // out_rows, out_rows)

        pl.run_scoped(cast_out, pltpu.VMEM((2, out_rows, D_MODEL), F32), pltpu.SemaphoreType.DMA((2,)))

        ltail[...] = jnp.zeros_like(ltail)
        stail[...] = jnp.zeros_like(stail)
        hcarry[...] = jnp.zeros_like(hcarry)
        state[...] = jnp.zeros_like(state)

    refs = (wout_bf, ltail, stail, hcarry, state)
    scratch = (a_s, h_s, ymix, ymix_nat, ys)
    reset = (m % tiles_per_seq) == 0
    x = x_ref[...]
    fill = _make_filler(_stage_a_pieces(x, perm, w_bf, wdt_bf, xp0, pset0, n_chunks))
    _stage_b(x, pset0, reset, o_ref, slice(0, ts), consts, refs, scratch, masks, n_chunks, fill)


def _channel_kernel(x_ref, p_ref, w1_hbm, w2_hbm, wgate_hbm, wple_hbm, g2_ref, b2_ref, g3_ref, b3_ref,
                    o_ref, w1_bf, w2_bf, wgate_bf, wple_bf, *, layer):
    @pl.when(pl.program_id(0) == 0)
    def _():
        def cast(w_hbm, dst_ref, rows):
            n_rows, n_cols = w_hbm.shape

            def body(stage, sem):
                _load_cast(w_hbm, _store_rows_bf16(dst_ref, rows), stage, sem, n_rows // rows, rows)
            pl.run_scoped(body, pltpu.VMEM((2, rows, n_cols), F32), pltpu.SemaphoreType.DMA((2,)))

        cast(w1_hbm.at[layer], w1_bf, 64)
        cast(w2_hbm.at[layer], w2_bf, 256)
        cast(wgate_hbm.at[layer], wgate_bf, 256)
        cast(wple_hbm.at[layer], wple_bf, 128)

    x = x_ref[...]
    hid = _dot(x.astype(BF16), w1_bf[...])
    hid = jnp.maximum(hid, 0.0)
    hid = (hid * hid).astype(BF16)
    ff = _dot(hid, w2_bf[...])
    x2 = _layer_norm(ALPHA * x + ff, g2_ref[...], b2_ref[...])
    gate = _sigmoid(_dot(x2.astype(BF16), wgate_bf[...]))
    ple = _dot(p_ref[...].astype(BF16), wple_bf[...])
    o_ref[...] = _layer_norm(ALPHA * x2 + gate * ple, g3_ref[...], b3_ref[...])


def _const_spec(shape):
    nd = len(shape)
    return pl.BlockSpec(shape, lambda *_: (0,) * nd, pipeline_mode=pl.Buffered(1))


def _block_diag_gates(wa, wx):
    def bd(w):
        w4 = w.reshape(LRU_WIDTH // GATE_BLOCK, GATE_BLOCK // LRU_HEAD_DIM, LRU_HEAD_DIM, LRU_HEAD_DIM)
        eye = jnp.eye(GATE_BLOCK // LRU_HEAD_DIM, dtype=w.dtype)
        out = jnp.einsum("jhio,hk->jhiko", w4, eye)
        return out.reshape(LRU_WIDTH // GATE_BLOCK, GATE_BLOCK, GATE_BLOCK)
    return jnp.concatenate([bd(wa), bd(wx)], axis=-1)


def _mixer_call(x2d, w_in, w_out, consts, *, ts, seq, layer):
    n_rows, d = x2d.shape
    n_tiles = n_rows // ts
    tiles_per_seq = seq // ts
    n_chunks = ts // CHUNK
    any_spec = pl.BlockSpec(memory_space=pl.ANY)
    in_specs = [pl.BlockSpec((ts, d), lambda m: (m, 0)),
                any_spec, any_spec] + [_const_spec(c.shape) for c in consts]
    handoff = [
        pltpu.VMEM((n_chunks, CBUF_ROWS, LRU_WIDTH), F32),
        pltpu.VMEM((n_chunks, CBUF_ROWS, SSD_XBC), F32),
        pltpu.VMEM((ts, LRU_WIDTH), F32),
        pltpu.VMEM((ts, SSD_WIDTH), F32),
        pltpu.VMEM((ts, LANES), F32),
    ]
    scratch = [
        pltpu.VMEM((W_MAIN_COLS, D_MODEL), BF16),
        pltpu.VMEM((LANES, D_MODEL), BF16),
        pltpu.VMEM((D_MIX, D_MODEL), BF16),
        pltpu.VMEM((HALO, LRU_WIDTH), F32),
        pltpu.VMEM((HALO, SSD_XBC), F32),
        pltpu.VMEM((SUBLANES, LRU_WIDTH), F32),
        pltpu.VMEM((SSD_GROUPS, SSD_STATE, GROUP_WIDTH), F32),
    ] + handoff + [
        pltpu.VMEM((ts, D_MODEL), BF16),
        pltpu.VMEM((ts, LRU_WIDTH), F32),
        pltpu.VMEM((ts, LRU_WIDTH), F32),
        pltpu.VMEM((ts, D_MIX), BF16),
        pltpu.VMEM((ts, D_MIX), BF16),
        pltpu.VMEM((ts, SSD_WIDTH), F32),
    ]
    return pl.pallas_call(
        functools.partial(_mixer_kernel, ts=ts, layer=layer, tiles_per_seq=tiles_per_seq),
        grid=(n_tiles,),
        in_specs=in_specs,
        out_specs=pl.BlockSpec((ts, d), lambda m: (m, 0)),
        out_shape=jax.ShapeDtypeStruct((n_rows, d), F32),
        scratch_shapes=scratch,
        compiler_params=pltpu.CompilerParams(
            dimension_semantics=("arbitrary",),
            vmem_limit_bytes=VMEM_LIMIT_BYTES),
        name="mixer",
    )(x2d, w_in, w_out, *consts)


def _channel_call(x1, p, w1, w2, wgate, wple, g2, b2, g3, b3, *, tm, layer):
    n, d = x1.shape
    any_spec = pl.BlockSpec(memory_space=pl.ANY)
    consts = (g2, b2, g3, b3)
    in_specs = [pl.BlockSpec((tm, d), lambda i: (i, 0)),
                pl.BlockSpec((None, tm, PLE_DIM), lambda i: (layer, i, 0)),
                any_spec, any_spec, any_spec, any_spec] + [_const_spec(c.shape) for c in consts]
    scratch = [
        pltpu.VMEM((D_MODEL, D_FF), BF16),
        pltpu.VMEM((D_FF, D_MODEL), BF16),
        pltpu.VMEM((D_MODEL, D_MODEL), BF16),
        pltpu.VMEM((PLE_DIM, D_MODEL), BF16),
    ]
    return pl.pallas_call(
        functools.partial(_channel_kernel, layer=layer),
        grid=(n // tm,),
        in_specs=in_specs,
        out_specs=pl.BlockSpec((tm, d), lambda i: (i, 0)),
        out_shape=jax.ShapeDtypeStruct((n, d), F32),
        scratch_shapes=scratch,
        compiler_params=pltpu.CompilerParams(
            dimension_semantics=("arbitrary",),
            vmem_limit_bytes=VMEM_LIMIT_BYTES),
        name="channel",
    )(x1, p, w1, w2, wgate, wple, *consts)


MIXER_TS = 512
CHANNEL_TM = 512


def kernel(x, p, w_in, lru_conv_w, lru_conv_b, lru_gate_a_w, lru_gate_a_b, lru_gate_x_w, lru_gate_x_b,
           lru_a_param, ssd_conv_w, ssd_conv_b, ssd_dt_bias, ssd_a_log, ssd_d, ssd_norm_w, w_out,
           ln1_g, ln1_b, w_ff1, w_ff2, ln2_g, ln2_b, w_ple_gate, w_ple, ln3_g, ln3_b):
    bsz, seq, d = x.shape
    row = lambda v: v.reshape(1, -1).astype(F32)
    pad_heads = lambda v: jnp.pad(v.reshape(1, -1).astype(F32), ((0, 0), (0, LANES - SSD_HEADS)))
    x2d = x.reshape(bsz * seq, d)
    for i in range(DEPTH):
        wg = _block_diag_gates(lru_gate_a_w[i], lru_gate_x_w[i]).astype(BF16)
        dexp = jnp.repeat(ssd_d[i].astype(F32), SSD_HEAD_DIM).reshape(1, -1)
        consts = (lru_conv_w[i].astype(F32), row(lru_conv_b[i]), wg,
                  row(lru_gate_a_b[i]), row(lru_gate_x_b[i]), row(lru_a_param[i]),
                  ssd_conv_w[i].astype(F32), row(ssd_conv_b[i]), pad_heads(ssd_dt_bias[i]),
                  pad_heads(ssd_a_log[i]), dexp, row(ssd_norm_w[i]), row(ln1_g[i]), row(ln1_b[i]))
        x1 = _mixer_call(x2d, jnp.swapaxes(w_in, 1, 2), w_out, consts, ts=MIXER_TS, seq=seq, layer=i)
        x2d = _channel_call(
            x1, p.reshape(DEPTH, bsz * seq, PLE_DIM), w_ff1, w_ff2, w_ple_gate, w_ple,
            row(ln2_g[i]), row(ln2_b[i]), row(ln3_g[i]), row(ln3_b[i]), tm=CHANNEL_TM, layer=i)
    return x2d.reshape(bsz, seq, d)
```

```python
import functools
import math

import jax
import jax.numpy as jnp
from jax import lax
from jax.experimental import pallas as pl
from jax.experimental.pallas import tpu as pltpu

F32 = jnp.float32
BF16 = jnp.bfloat16

D_MODEL = 1024
PLE_DIM = 256
LRU_WIDTH = 1024
LRU_HEADS = 16
LRU_HEAD_DIM = 64
LRU_C = 8.0
SSD_WIDTH = 1024
SSD_HEAD_DIM = 64
SSD_HEADS = 16
SSD_GROUPS = 4
SSD_STATE = 128
SSD_CHUNK = 128
SSD_XBC = SSD_WIDTH + 2 * SSD_GROUPS * SSD_STATE
CONV_WIDTH = 4
D_FF = 4 * D_MODEL
D_MIX = LRU_WIDTH + SSD_WIDTH
DEPTH = 1
ALPHA = (2.0 * DEPTH) ** 0.25
LN_EPS = 1e-5
RMS_EPS = 1e-5

LANES = 128
SUBLANES = 8
GATE_BLOCK = 256
GROUP_WIDTH = SSD_WIDTH // SSD_GROUPS
HEADS_PER_GROUP = SSD_HEADS // SSD_GROUPS
VMEM_LIMIT_BYTES = 58 * 1024 * 1024

CHUNK = SSD_CHUNK
SEG_LEN = CHUNK // SUBLANES
N_SLABS = CHUNK // SUBLANES
HALO = (CONV_WIDTH - 1) * SUBLANES
CBUF_ROWS = HALO + CHUNK

OFF_XL = 0
OFF_GL = LRU_WIDTH
OFF_Z = 2 * LRU_WIDTH
OFF_XBC = 2 * LRU_WIDTH + SSD_WIDTH
W_MAIN_COLS = OFF_XBC + SSD_XBC
D_IN_PROJ = W_MAIN_COLS + SSD_HEADS


def _dot(a, b):
    return jnp.dot(a, b, preferred_element_type=F32)


def _dot_nt(a, b_t):
    return lax.dot_general(a, b_t, (((1,), (1,)), ((), ())), preferred_element_type=F32)


def _sigmoid(x):
    return 0.5 + 0.5 * jnp.tanh(0.5 * x)


def _silu(x):
    hx = 0.5 * x
    return hx + hx * jnp.tanh(hx)


def _softplus(x):
    return jnp.maximum(x, 0.0) + jnp.log1p(jnp.exp(-jnp.abs(x)))


def _gelu_tanh(x):
    c = math.sqrt(2.0 / math.pi)
    return 0.5 * x * (1.0 + jnp.tanh(c * (x + 0.044715 * (x * x * x))))


def _layer_norm(v, g, b):
    mu = jnp.mean(v, axis=-1, keepdims=True)
    vc = v - mu
    var = jnp.mean(vc * vc, axis=-1, keepdims=True)
    return vc * lax.rsqrt(var + LN_EPS) * g + b


def _masked_prefix_sum(mask_bf, a):
    hi = a.astype(BF16)
    rest = a - hi.astype(F32)
    mid = rest.astype(BF16)
    lo = (rest - mid.astype(F32)).astype(BF16)
    return _dot(mask_bf, hi) + _dot(mask_bf, mid) + _dot(mask_bf, lo)


def _load_cast(w_hbm, dst_fn, stage, sem, n_chunks, rows):
    def copy(c):
        return pltpu.make_async_copy(w_hbm.at[pl.ds(c * rows, rows), :], stage.at[c % 2], sem.at[c % 2])
    copy(0).start()
    for c in range(n_chunks):
        if c + 1 < n_chunks:
            copy(c + 1).start()
        copy(c).wait()
        dst_fn(c, stage[c % 2])


def _store_rows_bf16(dst_ref, rows):
    def fn(c, val):
        dst_ref[c * rows:(c + 1) * rows, :] = val.astype(BF16)
    return fn


def _perm_conv(pbuf, tail_val, tail, w_ref, b_ref, n_chunks, sub_eq_last, cols):
    outs = []
    for c in range(n_chunks):
        prev3 = tail_val[:, cols] if c == 0 else pbuf[c - 1, CHUNK:CHUNK + HALO, cols]
        own3 = pbuf[c, CHUNK:CHUNK + HALO, cols]
        for m in range(CONV_WIDTH - 1):
            sl = slice(m * SUBLANES, (m + 1) * SUBLANES)
            merged = jnp.where(sub_eq_last, prev3[sl, :], own3[sl, :])
            pbuf[c, sl, cols] = pltpu.roll(merged, 1, 0)
        acc = b_ref[:, cols] + pbuf[c, 0:CHUNK, cols] * w_ref[0:1, cols]
        for k in range(1, CONV_WIDTH):
            acc = acc + pbuf[c, k * SUBLANES:k * SUBLANES + CHUNK, cols] * w_ref[k:k + 1, cols]
        outs.append(acc)
    tail[:, cols] = pbuf[n_chunks - 1, CHUNK:CHUNK + HALO, cols]
    return outs


def _expand_pair(v, h0, lane_lt_half):
    return jnp.where(lane_lt_half, v[:, h0:h0 + 1], v[:, h0 + 1:h0 + 2])


A_PIECE_COLS = 512


def _stage_a_pieces(x, perm, w_bf, wdt_bf, xp_s, pset, n_chunks):
    pxl, pxbc, pg, pz, pdt = pset

    def permute():
        xb = x.astype(BF16)
        for c in range(n_chunks):
            rs = slice(c * CHUNK, (c + 1) * CHUNK)
            xp_s[rs, :] = _dot(perm, xb[rs, :]).astype(BF16)

    def project(col0):
        def run():
            res = _dot_nt(xp_s[...], w_bf[col0:col0 + A_PIECE_COLS, :])
            if col0 < OFF_GL:
                for c in range(n_chunks):
                    pxl[c, HALO:HALO + CHUNK, col0:col0 + A_PIECE_COLS] = res[c * CHUNK:(c + 1) * CHUNK, :]
            elif col0 < OFF_Z:
                pg[:, col0 - OFF_GL:col0 - OFF_GL + A_PIECE_COLS] = res
            elif col0 < OFF_XBC:
                pz[:, col0 - OFF_Z:col0 - OFF_Z + A_PIECE_COLS] = res
            else:
                for c in range(n_chunks):
                    pxbc[c, HALO:HALO + CHUNK, col0 - OFF_XBC:col0 - OFF_XBC + A_PIECE_COLS] = (
                        res[c * CHUNK:(c + 1) * CHUNK, :])
        return run

    def project_dt():
        pdt[...] = _dot_nt(xp_s[...], wdt_bf[...])

    def cols(off, width):
        return [project(c0) for c0 in range(off, off + width, A_PIECE_COLS)]

    return ([permute] + cols(OFF_XL, LRU_WIDTH) + cols(OFF_GL, LRU_WIDTH) + cols(OFF_XBC, SSD_XBC)
            + [project_dt] + cols(OFF_Z, SSD_WIDTH))


def _make_filler(pieces):
    pending = list(pieces)

    def fill(n=None):
        count = len(pending) if n is None else min(n, len(pending))
        for _ in range(count):
            pending.pop(0)()
    return fill


def _stage_b(x, pset, reset, out_ref, out_rows, consts, refs, scratch, masks, n_chunks, fill):
    pxl, pxbc, pg, pz, pdt = pset
    (lcw_ref, lcb_ref, wg_ref, ba_ref, bx_ref, ap_ref, scw_ref, scb_ref, dtb_ref, alog_ref, dexp_ref, nw_ref,
     g1_ref, b1_ref) = consts
    wout_bf, ltail, stail, hcarry, state = refs
    a_s, h_s, ymix, ymix_nat, ys = scratch
    perm_t, causal, tril, sub, sub_eq_last, sub_eq_first, lane_lt_half, lane_row = masks
    ts = n_chunks * CHUNK

    def carried(ref_val):
        return ref_val if reset is None else jnp.where(reset, 0.0, ref_val)

    row = lax.broadcasted_iota(jnp.int32, (ts, 1), 0)
    if reset is None:
        is_first = None
    else:
        is_first = row == jnp.where(reset, 0, -1)

    fill(1 + LRU_WIDTH // A_PIECE_COLS)
    xc = jnp.concatenate(
        _perm_conv(pxl, carried(ltail[...]), ltail, lcw_ref, lcb_ref, n_chunks, sub_eq_last,
                   slice(0, LRU_WIDTH)), axis=0)
    xcb = xc.astype(BF16)
    for jb in range(LRU_WIDTH // GATE_BLOCK):
        cs = slice(jb * GATE_BLOCK, (jb + 1) * GATE_BLOCK)
        gr = _dot(xcb[:, cs], wg_ref[jb])
        r = _sigmoid(gr[:, :GATE_BLOCK] + ba_ref[:, cs])
        ig = _sigmoid(gr[:, GATE_BLOCK:] + bx_ref[:, cs])
        log_a = r * ((-LRU_C) * _softplus(-ap_ref[:, cs]))
        a = jnp.exp(log_a)
        one_m_a2 = 1.0 - a * a
        mult = jnp.where(one_m_a2 > 0.0, one_m_a2 * lax.rsqrt(one_m_a2), 0.0)
        if is_first is not None:
            mult = jnp.where(is_first, 1.0, mult)
        u = xc[:, cs] * ig * mult
        h0 = carried(hcarry[:, cs])
        for c in range(n_chunks):
            r0 = c * CHUNK
            h_run = u[r0:r0 + SUBLANES, :]
            a_run = a[r0:r0 + SUBLANES, :]
            h_s[r0:r0 + SUBLANES, cs] = h_run
            a_s[r0:r0 + SUBLANES, cs] = a_run
            for j in range(1, N_SLABS):
                rj = slice(r0 + j * SUBLANES, r0 + (j + 1) * SUBLANES)
                a_j = a[rj, :]
                h_run = a_j * h_run + u[rj, :]
                a_run = a_j * a_run
                h_s[rj, cs] = h_run
                a_s[rj, cs] = a_run
            a_seg, u_seg = a_run, h_run
            for s in (1, 2, 4):
                valid = sub >= s
                a_sh = jnp.where(valid, pltpu.roll(a_seg, s, 0), 1.0)
                u_sh = jnp.where(valid, pltpu.roll(u_seg, s, 0), 0.0)
                u_seg = a_seg * u_sh + u_seg
                a_seg = a_seg * a_sh
            h_end = a_seg * h0 + u_seg
            h_in = jnp.where(sub_eq_first, h0, pltpu.roll(h_end, 1, 0))
            h0 = jnp.broadcast_to(h_end[SUBLANES - 1:SUBLANES, :], (SUBLANES, GATE_BLOCK))
            for j in range(N_SLABS):
                rj = slice(r0 + j * SUBLANES, r0 + (j + 1) * SUBLANES)
                h_s[rj, cs] = h_s[rj, cs] + a_s[rj, cs] * h_in
        hcarry[:, cs] = h0

    fill(LRU_WIDTH // A_PIECE_COLS)
    ymix[:, 0:LRU_WIDTH] = (_gelu_tanh(pg[...]) * h_s[...]).astype(BF16)

    stail_val = carried(stail[...])
    xbc_blocks = []
    for cb0 in range(0, SSD_XBC, A_PIECE_COLS):
        fill(1)
        blk = _perm_conv(pxbc, stail_val, stail, scw_ref, scb_ref, n_chunks, sub_eq_last,
                         slice(cb0, cb0 + A_PIECE_COLS))
        xbc_blocks.append([_silu(v) for v in blk])
    xbc_chunks = [jnp.concatenate([blk[c] for blk in xbc_blocks], axis=1) for c in range(n_chunks)]
    fill(1)
    dt = _softplus(pdt[...] + dtb_ref[...])
    a_neg = -jnp.exp(alog_ref[...])
    a_dt = dt * a_neg

    for c in range(n_chunks):
        rs = slice(c * CHUNK, (c + 1) * CHUNK)
        xbc = xbc_chunks[c]
        a_c = a_dt[rs, :]
        dt_c = dt[rs, :]
        acs = _masked_prefix_sum(tril, a_c)
        acs_t = acs.T
        dt_t = dt_c.T
        acs_last = acs[CHUNK - 1:CHUNK, :]
        e_acs = jnp.exp(acs)
        w_st = dt_c * jnp.exp(acs_last - acs)
        e_last = jnp.exp(acs_last)
        for g in range(SSD_GROUPS):
            xcol = slice(g * GROUP_WIDTH, (g + 1) * GROUP_WIDTH)
            bcol = slice(SSD_WIDTH + g * SSD_STATE, SSD_WIDTH + (g + 1) * SSD_STATE)
            ccol = slice(SSD_WIDTH + SSD_GROUPS * SSD_STATE + g * SSD_STATE,
                         SSD_WIDTH + SSD_GROUPS * SSD_STATE + (g + 1) * SSD_STATE)
            xs_g = xbc[:, xcol]
            b_g = xbc[:, bcol]
            c_g = xbc[:, ccol]
            b_gb = b_g.astype(BF16)
            c_gb = c_g.astype(BF16)
            xs_gb = xs_g.astype(BF16)
            cb = _dot_nt(c_gb, b_gb)
            y_halves = []
            e_halves = []
            w_halves = []
            for p in range(HEADS_PER_GROUP // 2):
                h0h = g * HEADS_PER_GROUP + 2 * p
                xs_pair = xs_gb[:, p * LANES:(p + 1) * LANES]
                yd = []
                for h in (h0h, h0h + 1):
                    seg = acs[:, h:h + 1] - acs_t[h:h + 1, :]
                    dec = jnp.exp(jnp.where(causal, seg, -jnp.inf))
                    sc = (cb * dec * dt_t[h:h + 1, :]).astype(BF16)
                    yd.append(_dot(sc, xs_pair))
                y_halves.append(jnp.where(lane_lt_half, yd[0], yd[1]))
                e_halves.append(_expand_pair(e_acs, h0h, lane_lt_half))
                w_halves.append(_expand_pair(w_st, h0h, lane_lt_half))
            y_diag = jnp.concatenate(y_halves, axis=1)
            e_x = jnp.concatenate(e_halves, axis=1)
            w_x = jnp.concatenate(w_halves, axis=1)
            st = state[g]
            if c == 0:
                st = carried(st)
            y_off = _dot(c_gb, st.astype(BF16)) * e_x
            ys[rs, xcol] = y_diag + y_off + xs_g * dexp_ref[:, xcol]
            xw = (xs_g * w_x).astype(BF16)
            s_new = _dot(b_g.T.astype(BF16), xw)
            h0g = g * HEADS_PER_GROUP
            cd = jnp.concatenate(
                [jnp.where(lane_row, e_last[:, h0g + 2 * p:h0g + 2 * p + 1],
                           e_last[:, h0g + 2 * p + 1:h0g + 2 * p + 2])
                 for p in range(HEADS_PER_GROUP // 2)], axis=1)
            state[g] = st * cd + s_new

    fill()
    yf = ys[...] * _silu(pz[...])
    for g in range(SSD_GROUPS):
        xcol = slice(g * GROUP_WIDTH, (g + 1) * GROUP_WIDTH)
        yg = yf[:, xcol]
        ms = jnp.mean(yg * yg, axis=-1, keepdims=True)
        yn = yg * lax.rsqrt(ms + RMS_EPS) * nw_ref[:, xcol]
        ymix[:, LRU_WIDTH + g * GROUP_WIDTH:LRU_WIDTH + (g + 1) * GROUP_WIDTH] = yn.astype(BF16)

    for c in range(n_chunks):
        rs = slice(c * CHUNK, (c + 1) * CHUNK)
        ymix_nat[rs, :] = _dot(perm_t, ymix[rs, :]).astype(BF16)
    mix = _dot(ymix_nat[...], wout_bf[...])
    out_ref[out_rows, :] = _layer_norm(ALPHA * x + mix, g1_ref[...], b1_ref[...])


def _mixer_kernel(x_ref, win_hbm, wout_hbm, *rest, ts, layer, tiles_per_seq):
    consts = rest[:14]
    o_ref = rest[14]
    (w_bf, wdt_bf, wout_bf, ltail, stail, hcarry, state,
     pxl0, pxbc0, pg0, pz0, pdt0, xp0, a_s, h_s, ymix, ymix_nat, ys) = rest[15:]
    pset0 = (pxl0, pxbc0, pg0, pz0, pdt0)
    m = pl.program_id(0)
    n_chunks = ts // CHUNK

    pr = lax.broadcasted_iota(jnp.int32, (CHUNK, CHUNK), 0)
    pc = lax.broadcasted_iota(jnp.int32, (CHUNK, CHUNK), 1)
    t_of_row = (pr % SUBLANES) * SEG_LEN + pr // SUBLANES
    t_of_col = (pc % SUBLANES) * SEG_LEN + pc // SUBLANES
    perm = (pc == t_of_row).astype(BF16)
    perm_t = (pr == t_of_col).astype(BF16)
    causal = t_of_col <= t_of_row
    tril = causal.astype(BF16)
    sub = lax.broadcasted_iota(jnp.int32, (SUBLANES, 1), 0)
    lane = lax.broadcasted_iota(jnp.int32, (CHUNK, LANES), 1)
    masks = (perm_t, causal, tril, sub, sub == SUBLANES - 1, sub == 0, lane < SSD_HEAD_DIM,
             lax.broadcasted_iota(jnp.int32, (1, LANES), 1) < SSD_HEAD_DIM)

    @pl.when(m == 0)
    def _():
        in_rows = 256
        out_rows = 256

        def cast_in(stage, sem):
            _load_cast(win_hbm.at[layer], _store_rows_bf16(w_bf, in_rows), stage, sem,
                       W_MAIN_COLS // in_rows, in_rows)

        pl.run_scoped(cast_in, pltpu.VMEM((2, in_rows, D_MODEL), F32), pltpu.SemaphoreType.DMA((2,)))

        def cast_dt(stage, sem):
            copy = pltpu.make_async_copy(win_hbm.at[layer, pl.ds(W_MAIN_COLS, SSD_HEADS), :], stage, sem)
            copy.start()
            wdt_bf[...] = jnp.zeros_like(wdt_bf)
            copy.wait()
            wdt_bf[0:SSD_HEADS, :] = stage[...].astype(BF16)

        pl.run_scoped(cast_dt, pltpu.VMEM((SSD_HEADS, D_MODEL), F32), pltpu.SemaphoreType.DMA(()))

        def cast_out(stage, sem):
            _load_cast(wout_hbm.at[layer], _store_rows_bf16(wout_bf, out_rows), stage, sem,
                       D_MIX // out_rows, out_rows)

        pl.run_scoped(cast_out, pltpu.VMEM((2, out_rows, D_MODEL), F32), pltpu.SemaphoreType.DMA((2,)))

        ltail[...] = jnp.zeros_like(ltail)
        stail[...] = jnp.zeros_like(stail)
        hcarry[...] = jnp.zeros_like(hcarry)
        state[...] = jnp.zeros_like(state)

    refs = (wout_bf, ltail, stail, hcarry, state)
    scratch = (a_s, h_s, ymix, ymix_nat, ys)
    reset = (m % tiles_per_seq) == 0
    x = x_ref[...]
    fill = _make_filler(_stage_a_pieces(x, perm, w_bf, wdt_bf, xp0, pset0, n_chunks))
    _stage_b(x, pset0, reset, o_ref, slice(0, ts), consts, refs, scratch, masks, n_chunks, fill)


def _channel_kernel(x_ref, p_ref, w1_hbm, w2_hbm, wgate_hbm, wple_hbm, g2_ref, b2_ref, g3_ref, b3_ref,
                    o_ref, w1_bf, w2_bf, wgate_bf, wple_bf, *, layer):
    @pl.when(pl.program_id(0) == 0)
    def _():
        def cast(w_hbm, dst_ref, rows):
            n_rows, n_cols = w_hbm.shape

            def body(stage, sem):
                _load_cast(w_hbm, _store_rows_bf16(dst_ref, rows), stage, sem, n_rows // rows, rows)
            pl.run_scoped(body, pltpu.VMEM((2, rows, n_cols), F32), pltpu.SemaphoreType.DMA((2,)))

        cast(w1_hbm.at[layer], w1_bf, 64)
        cast(w2_hbm.at[layer], w2_bf, 256)
        cast(wgate_hbm.at[layer], wgate_bf, 256)
        cast(wple_hbm.at[layer], wple_bf, 128)

    x = x_ref[...]
    hid = _dot(x.astype(BF16), w1_bf[...])
    hid = jnp.maximum(hid, 0.0)
    hid = (hid * hid).astype(BF16)
    ff = _dot(hid, w2_bf[...])
    x2 = _layer_norm(ALPHA * x + ff, g2_ref[...], b2_ref[...])
    gate = _sigmoid(_dot(x2.astype(BF16), wgate_bf[...]))
    ple = _dot(p_ref[...].astype(BF16), wple_bf[...])
    o_ref[...] = _layer_norm(ALPHA * x2 + gate * ple, g3_ref[...], b3_ref[...])


def _const_spec(shape):
    nd = len(shape)
    return pl.BlockSpec(shape, lambda *_: (0,) * nd, pipeline_mode=pl.Buffered(1))


def _block_diag_gates(wa, wx):
    def bd(w):
        w4 = w.reshape(LRU_WIDTH // GATE_BLOCK, GATE_BLOCK // LRU_HEAD_DIM, LRU_HEAD_DIM, LRU_HEAD_DIM)
        eye = jnp.eye(GATE_BLOCK // LRU_HEAD_DIM, dtype=w.dtype)
        out = jnp.einsum("jhio,hk->jhiko", w4, eye)
        return out.reshape(LRU_WIDTH // GATE_BLOCK, GATE_BLOCK, GATE_BLOCK)
    return jnp.concatenate([bd(wa), bd(wx)], axis=-1)


def _mixer_call(x2d, w_in, w_out, consts, *, ts, seq, layer):
    n_rows, d = x2d.shape
    n_tiles = n_rows // ts
    tiles_per_seq = seq // ts
    n_chunks = ts // CHUNK
    any_spec = pl.BlockSpec(memory_space=pl.ANY)
    in_specs = [pl.BlockSpec((ts, d), lambda m: (m, 0)),
                any_spec, any_spec] + [_const_spec(c.shape) for c in consts]
    handoff = [
        pltpu.VMEM((n_chunks, CBUF_ROWS, LRU_WIDTH), F32),
        pltpu.VMEM((n_chunks, CBUF_ROWS, SSD_XBC), F32),
        pltpu.VMEM((ts, LRU_WIDTH), F32),
        pltpu.VMEM((ts, SSD_WIDTH), F32),
        pltpu.VMEM((ts, LANES), F32),
    ]
    scratch = [
        pltpu.VMEM((W_MAIN_COLS, D_MODEL), BF16),
        pltpu.VMEM((LANES, D_MODEL), BF16),
        pltpu.VMEM((D_MIX, D_MODEL), BF16),
        pltpu.VMEM((HALO, LRU_WIDTH), F32),
        pltpu.VMEM((HALO, SSD_XBC), F32),
        pltpu.VMEM((SUBLANES, LRU_WIDTH), F32),
        pltpu.VMEM((SSD_GROUPS, SSD_STATE, GROUP_WIDTH), F32),
    ] + handoff + [
        pltpu.VMEM((ts, D_MODEL), BF16),
        pltpu.VMEM((ts, LRU_WIDTH), F32),
        pltpu.VMEM((ts, LRU_WIDTH), F32),
        pltpu.VMEM((ts, D_MIX), BF16),
        pltpu.VMEM((ts, D_MIX), BF16),
        pltpu.VMEM((ts, SSD_WIDTH), F32),
    ]
    return pl.pallas_call(
        functools.partial(_mixer_kernel, ts=ts, layer=layer, tiles_per_seq=tiles_per_seq),
        grid=(n_tiles,),
        in_specs=in_specs,
        out_specs=pl.BlockSpec((ts, d), lambda m: (m, 0)),
        out_shape=jax.ShapeDtypeStruct((n_rows, d), F32),
        scratch_shapes=scratch,
        compiler_params=pltpu.CompilerParams(
            dimension_semantics=("arbitrary",),
            vmem_limit_bytes=VMEM_LIMIT_BYTES),
        name="mixer",
    )(x2d, w_in, w_out, *consts)


def _channel_call(x1, p, w1, w2, wgate, wple, g2, b2, g3, b3, *, tm, layer):
    n, d = x1.shape
    any_spec = pl.BlockSpec(memory_space=pl.ANY)
    consts = (g2, b2, g3, b3)
    in_specs = [pl.BlockSpec((tm, d), lambda i: (i, 0)),
                pl.BlockSpec((None, tm, PLE_DIM), lambda i: (layer, i, 0)),
                any_spec, any_spec, any_spec, any_spec] + [_const_spec(c.shape) for c in consts]
    scratch = [
        pltpu.VMEM((D_MODEL, D_FF), BF16),
        pltpu.VMEM((D_FF, D_MODEL), BF16),
        pltpu.VMEM((D_MODEL, D_MODEL), BF16),
        pltpu.VMEM((PLE_DIM, D_MODEL), BF16),
    ]
    return pl.pallas_call(
        functools.partial(_channel_kernel, layer=layer),
        grid=(n // tm,),
        in_specs=in_specs,
        out_specs=pl.BlockSpec((tm, d), lambda i: (i, 0)),
        out_shape=jax.ShapeDtypeStruct((n, d), F32),
        scratch_shapes=scratch,
        compiler_params=pltpu.CompilerParams(
            dimension_semantics=("arbitrary",),
            vmem_limit_bytes=VMEM_LIMIT_BYTES),
        name="channel",
    )(x1, p, w1, w2, wgate, wple, *consts)


MIXER_TS = 512
CHANNEL_TM = 512


def kernel(x, p, w_in, lru_conv_w, lru_conv_b, lru_gate_a_w, lru_gate_a_b, lru_gate_x_w, lru_gate_x_b,
           lru_a_param, ssd_conv_w, ssd_conv_b, ssd_dt_bias, ssd_a_log, ssd_d, ssd_norm_w, w_out,
           ln1_g, ln1_b, w_ff1, w_ff2, ln2_g, ln2_b, w_ple_gate, w_ple, ln3_g, ln3_b):
    bsz, seq, d = x.shape
    row = lambda v: v.reshape(1, -1).astype(F32)
    pad_heads = lambda v: jnp.pad(v.reshape(1, -1).astype(F32), ((0, 0), (0, LANES - SSD_HEADS)))
    x2d = x.reshape(bsz * seq, d)
    for i in range(DEPTH):
        wg = _block_diag_gates(lru_gate_a_w[i], lru_gate_x_w[i]).astype(BF16)
        dexp = jnp.repeat(ssd_d[i].astype(F32), SSD_HEAD_DIM).reshape(1, -1)
        consts = (lru_conv_w[i].astype(F32), row(lru_conv_b[i]), wg,
                  row(lru_gate_a_b[i]), row(lru_gate_x_b[i]), row(lru_a_param[i]),
                  ssd_conv_w[i].astype(F32), row(ssd_conv_b[i]), pad_heads(ssd_dt_bias[i]),
                  pad_heads(ssd_a_log[i]), dexp, row(ssd_norm_w[i]), row(ln1_g[i]), row(ln1_b[i]))
        x1 = _mixer_call(x2d, jnp.swapaxes(w_in, 1, 2), w_out, consts, ts=MIXER_TS, seq=seq, layer=i)
        x2d = _channel_call(
            x1, p.reshape(DEPTH, bsz * seq, PLE_DIM), w_ff1, w_ff2, w_ple_gate, w_ple,
            row(ln2_g[i]), row(ln2_b[i]), row(ln3_g[i]), row(ln3_b[i]), tm=CHANNEL_TM, layer=i)
    return x2d.reshape(bsz, seq, d)
```
